```python
import math, functools
import jax, jax.numpy as jnp
from jax import lax
import numpy as np

D_MODEL = 1024
BATCH = 4
SEQ = 4096
DEPTH = 1
DEC_BATCH = 128
DEC_SEQ = 1
PAST_LEN = 2048
PAGE_SIZE = 128

SSM_WIDTH = D_MODEL // 2
SSM_GROUP = 16
SSM_GROUPS = SSM_WIDTH // SSM_GROUP
SSM_STATE = 64
HEAD_DIM = 64
N_HEADS = D_MODEL // (2 * HEAD_DIM)
QK_DIM = 2 * HEAD_DIM
V_DIM = 2 * HEAD_DIM
ATTN_WIDTH = N_HEADS * QK_DIM
ROPE_DIM = HEAD_DIM // 4
ROPE_THETA = 500000.0
Q_BLOCK = 128
D_FF = -(-8 * D_MODEL // (3 * 256)) * 256
IN_WIDTH = SSM_WIDTH + 3 * ATTN_WIDTH + 2 * D_MODEL
EPS = 1e-6
F32 = jnp.float32

kernel_name = "hybrid_s5_diffattn_step"


def rmsnorm(x, g):
    x32 = x.astype(F32)
    r = x32 * lax.rsqrt(jnp.mean(x32 * x32, axis=-1, keepdims=True) + EPS)
    return (r * g.astype(F32)).astype(x.dtype)


def rope_partial(x, pos):
    half = ROPE_DIM // 2
    inv = ROPE_THETA ** (-jnp.arange(half, dtype=F32) * 2.0 / ROPE_DIM)
    ang = pos.astype(F32)[:, None] * inv[None, :]
    cos = jnp.cos(ang)[:, None, None, :]
    sin = jnp.sin(ang)[:, None, None, :]
    xr = x[..., :ROPE_DIM].astype(F32)
    x1, x2 = xr[..., :half], xr[..., half:]
    rot = jnp.concatenate([x1 * cos - x2 * sin, x2 * cos + x1 * sin], axis=-1).astype(x.dtype)
    return jnp.concatenate([rot, x[..., ROPE_DIM:]], axis=-1)


def diff_attn_core(q, k, v, q_pos, k_pos, lam):
    s = jnp.einsum("bqhcd,bkhcd->bhcqk", q, k, preferred_element_type=F32) * (HEAD_DIM ** -0.5)
    mask = k_pos[None, :] <= q_pos[:, None]
    s = jnp.where(mask, s, -jnp.inf)
    p = jax.nn.softmax(s, axis=-1)
    w = p[:, :, 0] - lam * p[:, :, 1]
    return jnp.einsum("bhqk,bkhe->bqhe", w.astype(v.dtype), v)


def prompt_attention(q, k, v, pos, lam):
    outs = []
    for s0 in range(0, q.shape[1], Q_BLOCK):
        e = s0 + Q_BLOCK
        outs.append(diff_attn_core(q[:, s0:e], k[:, :e], v[:, :e], pos[s0:e], pos[:e], lam))
    return jnp.concatenate(outs, axis=1)


def cached_attention(q, k, v, pos, lam, k_past, v_past):
    past = k_past.shape[1]
    k_all = jnp.concatenate([k_past.astype(k.dtype), k], axis=1)
    v_all = jnp.concatenate([v_past.astype(v.dtype), v], axis=1)
    k_pos = jnp.concatenate([jnp.arange(past), pos])
    return diff_attn_core(q, k_all, v_all, pos, k_pos, lam)


def ssm_branch(u, h0_re, h0_im, lam_re, lam_im, log_dt, b_re, b_im, c_re, c_im, d_skip):
    lam_re, lam_im = lam_re.astype(F32), lam_im.astype(F32)
    b_re, b_im = b_re.astype(F32), b_im.astype(F32)
    dt = jnp.exp(log_dt.astype(F32))[:, None]
    mag = jnp.exp(lam_re * dt)
    ar, ai = mag * jnp.cos(lam_im * dt), mag * jnp.sin(lam_im * dt)
    den = lam_re * lam_re + lam_im * lam_im
    fr = ((ar - 1.0) * lam_re + ai * lam_im) / den
    fi = (ai * lam_re - (ar - 1.0) * lam_im) / den
    bb_re = fr[..., None] * b_re - fi[..., None] * b_im
    bb_im = fr[..., None] * b_im + fi[..., None] * b_re
    B, T, _ = u.shape
    u32 = u.astype(F32)
    ug = u32.reshape(B, T, SSM_GROUPS, SSM_GROUP)
    bu_re = jnp.einsum("gpc,btgc->btgp", bb_re, ug)
    bu_im = jnp.einsum("gpc,btgc->btgp", bb_im, ug)
    a_re = jnp.broadcast_to(ar, bu_re.shape)
    a_im = jnp.broadcast_to(ai, bu_re.shape)

    def combine(e1, e2):
        a1r, a1i, b1r, b1i = e1
        a2r, a2i, b2r, b2i = e2
        return (a2r * a1r - a2i * a1i, a2r * a1i + a2i * a1r,
                a2r * b1r - a2i * b1i + b2r, a2r * b1i + a2i * b1r + b2i)

    Ar, Ai, Hr, Hi = lax.associative_scan(combine, (a_re, a_im, bu_re, bu_im), axis=1)
    h0r = h0_re.astype(F32)[:, None]
    h0i = h0_im.astype(F32)[:, None]
    hr = Ar * h0r - Ai * h0i + Hr
    hi = Ar * h0i + Ai * h0r + Hi
    y = (jnp.einsum("gcp,btgp->btgc", c_re.astype(F32), hr)
         - jnp.einsum("gcp,btgp->btgc", c_im.astype(F32), hi))
    y = y.reshape(B, T, SSM_WIDTH) + d_skip.astype(F32) * u32
    return y, hr[:, -1], hi[:, -1]


def trunk_layer(x, pos, h0_re, h0_im, attend, lam_init,
                norm_pre_mix, w_in, ssm_lambda_re, ssm_lambda_im, ssm_log_dt, ssm_b_re, ssm_b_im,
                ssm_c_re, ssm_c_im, ssm_d, w_glu_a, w_glu_b, lambda_q1, lambda_k1, lambda_q2,
                lambda_k2, subln_gain, w_o, norm_post_mix, norm_pre_ffn, w_gate, w_up, w_down,
                norm_post_ffn):
    B, T, _ = x.shape
    h = rmsnorm(x, norm_pre_mix)
    z = h @ w_in
    o0 = SSM_WIDTH
    o1 = o0 + ATTN_WIDTH
    o2 = o1 + ATTN_WIDTH
    o3 = o2 + ATTN_WIDTH
    o4 = o3 + D_MODEL
    u, q, k, v = z[..., :o0], z[..., o0:o1], z[..., o1:o2], z[..., o2:o3]
    g_s, g_a = z[..., o3:o4], z[..., o4:]
    y_s, hT_re, hT_im = ssm_branch(u, h0_re, h0_im, ssm_lambda_re, ssm_lambda_im, ssm_log_dt,
                                   ssm_b_re, ssm_b_im, ssm_c_re, ssm_c_im, ssm_d)
    y_s = jax.nn.gelu(y_s).astype(x.dtype)
    y_s = (y_s @ w_glu_a) * jax.nn.sigmoid(y_s @ w_glu_b)
    q = rope_partial(q.reshape(B, T, N_HEADS, 2, HEAD_DIM), pos)
    k = rope_partial(k.reshape(B, T, N_HEADS, 2, HEAD_DIM), pos)
    v = v.reshape(B, T, N_HEADS, V_DIM)
    lam = (jnp.exp(jnp.sum(lambda_q1.astype(F32) * lambda_k1.astype(F32)))
           - jnp.exp(jnp.sum(lambda_q2.astype(F32) * lambda_k2.astype(F32))) + lam_init)
    o = attend(q, k, v, pos, lam)
    o = (rmsnorm(o, subln_gain) * (1.0 - lam_init)).astype(x.dtype).reshape(B, T, ATTN_WIDTH)
    mixed = jax.nn.sigmoid(g_s) * y_s + jax.nn.sigmoid(g_a) * o
    x = x + rmsnorm(mixed @ w_o, norm_post_mix)
    hf = rmsnorm(x, norm_pre_ffn)
    f = (jax.nn.silu(hf @ w_gate) * (hf @ w_up)) @ w_down
    x = x + rmsnorm(f, norm_post_ffn)
    k_rows = k.reshape(B, T, N_HEADS, QK_DIM)
    return x, k_rows, v, hT_re, hT_im


def setup_inputs(seed: int = 0) -> dict:
    key = jax.random.key(seed)
    ks = jax.random.split(key, 32)
    n_pages = PAST_LEN // PAGE_SIZE
    n_used = DEC_BATCH * n_pages
    n_phys = (5 * n_used + 3) // 4
    nrm = lambda k, shape, s: jax.random.normal(k, shape, F32) * s
    page_table = jax.random.permutation(ks[0], n_phys)[:n_used].reshape(DEC_BATCH, n_pages).astype(jnp.int32)
    n_idx = jnp.arange(SSM_STATE, dtype=F32)
    lam_re = -0.5 + nrm(ks[1], (DEPTH, SSM_GROUPS, SSM_STATE), 0.01)
    lam_im = math.pi * n_idx[None, None, :] + nrm(ks[2], (DEPTH, SSM_GROUPS, SSM_STATE), 0.01)
    log_dt = jax.random.uniform(ks[3], (DEPTH, SSM_GROUPS), F32, math.log(1e-3), math.log(1e-1))
    gain = lambda k, n: 1.0 + nrm(k, (DEPTH, n), 0.02)
    return {
        "x_prompt": nrm(ks[4], (BATCH, SEQ, D_MODEL), 1.0),
        "x_sample": nrm(ks[5], (DEC_BATCH, DEC_SEQ, D_MODEL), 1.0),
        "cache_k": nrm(ks[6], (DEPTH, n_phys, PAGE_SIZE, N_HEADS, QK_DIM), 1.0),
        "cache_v": nrm(ks[7], (DEPTH, n_phys, PAGE_SIZE, N_HEADS, V_DIM), 1.0),
        "state_ssm_re": nrm(ks[8], (DEPTH, DEC_BATCH, SSM_GROUPS, SSM_STATE), 0.1),
        "state_ssm_im": nrm(ks[9], (DEPTH, DEC_BATCH, SSM_GROUPS, SSM_STATE), 0.1),
        "page_table": page_table,
        "norm_pre_mix": gain(ks[10], D_MODEL),
        "w_in": nrm(ks[11], (DEPTH, D_MODEL, IN_WIDTH), D_MODEL ** -0.5),
        "ssm_lambda_re": lam_re,
        "ssm_lambda_im": lam_im,
        "ssm_log_dt": log_dt,
        "ssm_b_re": nrm(ks[12], (DEPTH, SSM_GROUPS, SSM_STATE, SSM_GROUP), (2 * SSM_GROUP) ** -0.5),
        "ssm_b_im": nrm(ks[13], (DEPTH, SSM_GROUPS, SSM_STATE, SSM_GROUP), (2 * SSM_GROUP) ** -0.5),
        "ssm_c_re": nrm(ks[14], (DEPTH, SSM_GROUPS, SSM_GROUP, SSM_STATE), (2 * SSM_STATE) ** -0.5),
        "ssm_c_im": nrm(ks[15], (DEPTH, SSM_GROUPS, SSM_GROUP, SSM_STATE), (2 * SSM_STATE) ** -0.5),
        "ssm_d": nrm(ks[16], (DEPTH, SSM_WIDTH), 1.0),
        "w_glu_a": nrm(ks[17], (DEPTH, SSM_WIDTH, D_MODEL), SSM_WIDTH ** -0.5),
        "w_glu_b": nrm(ks[18], (DEPTH, SSM_WIDTH, D_MODEL), SSM_WIDTH ** -0.5),
        "lambda_q1": nrm(ks[19], (DEPTH, HEAD_DIM), 0.1),
        "lambda_k1": nrm(ks[20], (DEPTH, HEAD_DIM), 0.1),
        "lambda_q2": nrm(ks[21], (DEPTH, HEAD_DIM), 0.1),
        "lambda_k2": nrm(ks[22], (DEPTH, HEAD_DIM), 0.1),
        "subln_gain": gain(ks[23], V_DIM),
        "w_o": nrm(ks[24], (DEPTH, D_MODEL, D_MODEL), D_MODEL ** -0.5),
        "norm_post_mix": gain(ks[25], D_MODEL),
        "norm_pre_ffn": gain(ks[26], D_MODEL),
        "w_gate": nrm(ks[27], (DEPTH, D_MODEL, D_FF), D_MODEL ** -0.5),
        "w_up": nrm(ks[28], (DEPTH, D_MODEL, D_FF), D_MODEL ** -0.5),
        "w_down": nrm(ks[29], (DEPTH, D_FF, D_MODEL), D_FF ** -0.5),
        "norm_post_ffn": gain(ks[30], D_MODEL),
    }


def reference(x_prompt, x_sample, cache_k, cache_v, state_ssm_re, state_ssm_im, page_table,
              norm_pre_mix, w_in, ssm_lambda_re, ssm_lambda_im, ssm_log_dt, ssm_b_re, ssm_b_im,
              ssm_c_re, ssm_c_im, ssm_d, w_glu_a, w_glu_b, lambda_q1, lambda_k1, lambda_q2,
              lambda_k2, subln_gain, w_o, norm_post_mix, norm_pre_ffn, w_gate, w_up, w_down,
              norm_post_ffn):
    n_pages = PAST_LEN // PAGE_SIZE
    bp, tp = x_prompt.shape[0], x_prompt.shape[1]
    bs, ts = x_sample.shape[0], x_sample.shape[1]
    pos_p = jnp.arange(tp)
    pos_s = PAST_LEN + jnp.arange(ts)
    zeros = jnp.zeros((bp, SSM_GROUPS, SSM_STATE), F32)
    yp, ys = x_prompt, x_sample
    kp_l, vp_l, hpr_l, hpi_l, ks_l, vs_l, hsr_l, hsi_l = [], [], [], [], [], [], [], []
    for l in range(DEPTH):
        lam_init = 0.8 - 0.6 * math.exp(-0.3 * l)
        lw = (norm_pre_mix[l], w_in[l], ssm_lambda_re[l], ssm_lambda_im[l], ssm_log_dt[l],
              ssm_b_re[l], ssm_b_im[l], ssm_c_re[l], ssm_c_im[l], ssm_d[l], w_glu_a[l], w_glu_b[l],
              lambda_q1[l], lambda_k1[l], lambda_q2[l], lambda_k2[l], subln_gain[l], w_o[l],
              norm_post_mix[l], norm_pre_ffn[l], w_gate[l], w_up[l], w_down[l], norm_post_ffn[l])
        yp, kp, vp, hpr, hpi = trunk_layer(yp, pos_p, zeros, zeros, prompt_attention, lam_init, *lw)
        k_past = cache_k[l][page_table].reshape(bs, n_pages * PAGE_SIZE, N_HEADS, 2, HEAD_DIM)
        v_past = cache_v[l][page_table].reshape(bs, n_pages * PAGE_SIZE, N_HEADS, V_DIM)
        attend_s = functools.partial(cached_attention, k_past=k_past, v_past=v_past)
        ys, k_s, v_s, hsr, hsi = trunk_layer(ys, pos_s, state_ssm_re[l], state_ssm_im[l], attend_s,
                                             lam_init, *lw)
        kp_l.append(kp); vp_l.append(vp); hpr_l.append(hpr); hpi_l.append(hpi)
        ks_l.append(k_s); vs_l.append(v_s); hsr_l.append(hsr); hsi_l.append(hsi)
    return (yp, ys,
            jnp.stack(kp_l), jnp.stack(vp_l), jnp.stack(hpr_l), jnp.stack(hpi_l),
            jnp.stack(ks_l), jnp.stack(vs_l), jnp.stack(hsr_l), jnp.stack(hsi_l))
```

```python
import functools
import math

import jax
import jax.numpy as jnp
from jax import lax
from jax.experimental import pallas as pl
from jax.experimental.pallas import tpu as pltpu

F32 = jnp.float32
BF16 = jnp.bfloat16

D_MODEL = 1024
PAST_LEN = 2048
PAGE_SIZE = 128
SSM_WIDTH = 512
SSM_GROUP = 16
SSM_GROUPS = 32
SSM_STATE = 64
SSM_LANES = SSM_GROUPS * SSM_STATE
HEAD_DIM = 64
N_HEADS = 8
QK_DIM = 128
ATTN_WIDTH = 1024
ROPE_DIM = 16
ROPE_THETA = 500000.0
D_FF = 2816
IN_WIDTH = SSM_WIDTH + 3 * ATTN_WIDTH + 2 * D_MODEL
EPS = 1e-6
LAM_INIT = 0.8 - 0.6 * math.exp(-0.3 * 0)

LANES = 128
SUBLANES = 8
VMEM_LIMIT = 56 * 1024 * 1024
DECODE_PAGES_PER_STEP = 8


def _params(sem):
    return pltpu.CompilerParams(dimension_semantics=sem, vmem_limit_bytes=VMEM_LIMIT)


def _resident(shape):
    return pl.BlockSpec(shape, lambda *_: (0,) * len(shape), pipeline_mode=pl.Buffered(1))


def _rms(x, g):
    return x * lax.rsqrt(jnp.mean(x * x, axis=-1, keepdims=True) + EPS) * g


def _dot(a, b):
    return jnp.dot(a, b, preferred_element_type=F32)


def _dot_nt(a, b):
    return lax.dot_general(a, b, (((1,), (1,)), ((), ())), preferred_element_type=F32)


def _prep_kernel(lre_ref, lim_ref, ldt_ref, bre_ref, bim_ref, lq1_ref, lk1_ref, lq2_ref, lk2_ref,
                 are_ref, aim_ref, bbre_ref, bbim_ref, lam_ref):
    lre, lim = lre_ref[...], lim_ref[...]
    dt = jnp.exp(ldt_ref[...])
    mag = jnp.exp(lre * dt)
    ar, ai = mag * jnp.cos(lim * dt), mag * jnp.sin(lim * dt)
    den = lre * lre + lim * lim
    fr = ((ar - 1.0) * lre + ai * lim) / den
    fi = (ai * lre - (ar - 1.0) * lim) / den
    are_ref[...] = ar
    aim_ref[...] = ai
    bre, bim = bre_ref[...], bim_ref[...]
    bbre_ref[...] = fr[:, None, :] * bre - fi[:, None, :] * bim
    bbim_ref[...] = fr[:, None, :] * bim + fi[:, None, :] * bre
    s1 = jnp.sum(lq1_ref[...] * lk1_ref[...], axis=-1, keepdims=True)
    s2 = jnp.sum(lq2_ref[...] * lk2_ref[...], axis=-1, keepdims=True)
    lam_ref[...] = jnp.exp(s1) - jnp.exp(s2) + LAM_INIT


def _prep(lam_re, lam_im, log_dt, b_re, b_im, lq1, lk1, lq2, lk2):
    g, p, c = SSM_GROUPS, SSM_STATE, SSM_GROUP
    out_shape = (jax.ShapeDtypeStruct((g, p), F32), jax.ShapeDtypeStruct((g, p), F32),
                 jax.ShapeDtypeStruct((g, c, p), F32), jax.ShapeDtypeStruct((g, c, p), F32),
                 jax.ShapeDtypeStruct((1, 1), F32))
    return pl.pallas_call(_prep_kernel, out_shape=out_shape, name="ssm_prep")(
        lam_re, lam_im, log_dt.reshape(g, 1), jnp.swapaxes(b_re, 1, 2), jnp.swapaxes(b_im, 1, 2),
        lq1.reshape(1, HEAD_DIM), lk1.reshape(1, HEAD_DIM), lq2.reshape(1, HEAD_DIM), lk2.reshape(1, HEAD_DIM))


def _block_diag(blocks):
    g, r, c = blocks.shape
    eye = jnp.eye(g, dtype=blocks.dtype)
    return (blocks[:, :, None, :] * eye[:, None, :, None]).reshape(g * r, g * c)


def _in_proj_kernel(x_ref, g_ref, w_ref, cos_ref, sa_ref, sb_ref,
                    u_ref, q_ref, kf_ref, kb_ref, vf_ref, vb_ref, gs_ref, ga_ref):
    h = _rms(x_ref[...], g_ref[...]).astype(BF16)
    o0 = SSM_WIDTH
    o1, o2, o3, o4 = o0 + ATTN_WIDTH, o0 + 2 * ATTN_WIDTH, o0 + 3 * ATTN_WIDTH, o0 + 3 * ATTN_WIDTH + D_MODEL

    u_ref[...] = _dot(h, w_ref[:, 0:o0]).astype(BF16)
    cos, sa, sb = cos_ref[...], sa_ref[...], sb_ref[...]

    def rope(z):
        return z * cos + pltpu.roll(z, LANES - ROPE_DIM // 2, 1) * sa + pltpu.roll(z, ROPE_DIM // 2, 1) * sb

    zq = _dot(h, w_ref[:, o0:o1])
    for hh in range(N_HEADS):
        sl = slice(hh * QK_DIM, (hh + 1) * QK_DIM)
        q_ref[:, sl] = (rope(zq[:, sl]) * (HEAD_DIM ** -0.5)).astype(BF16)
    zk = _dot(h, w_ref[:, o1:o2])
    for hh in range(N_HEADS):
        sl = slice(hh * QK_DIM, (hh + 1) * QK_DIM)
        kr = rope(zk[:, sl])
        kf_ref[:, hh, :] = kr
        kb_ref[:, sl] = kr.astype(BF16)
    zv = _dot(h, w_ref[:, o2:o3])
    for hh in range(N_HEADS):
        vf_ref[:, hh, :] = zv[:, hh * QK_DIM:(hh + 1) * QK_DIM]
    vb_ref[...] = zv.astype(BF16)
    gs_ref[...] = jax.nn.sigmoid(_dot(h, w_ref[:, o3:o4])).astype(BF16)
    ga_ref[...] = jax.nn.sigmoid(_dot(h, w_ref[:, o4:IN_WIDTH])).astype(BF16)


def _rope_tables(pos):
    half = ROPE_DIM // 2
    inv = ROPE_THETA ** (-jnp.arange(half, dtype=F32) * 2.0 / ROPE_DIM)
    ang = pos.astype(F32)[:, None] * inv[None, :]
    cos, sin = jnp.cos(ang), jnp.sin(ang)
    n = pos.shape[0]
    comp_cos = jnp.concatenate([cos, cos, jnp.ones((n, HEAD_DIM - ROPE_DIM), F32)], axis=1)
    comp_sa = jnp.concatenate([-sin, jnp.zeros((n, HEAD_DIM - half), F32)], axis=1)
    comp_sb = jnp.concatenate([jnp.zeros((n, half), F32), sin, jnp.zeros((n, HEAD_DIM - ROPE_DIM), F32)], axis=1)
    two = lambda t: jnp.concatenate([t, t], axis=1)
    return two(comp_cos), two(comp_sa), two(comp_sb)


def _in_proj(x2d, pos, nb, g_pre, w_in_bf, tm):
    m = x2d.shape[0]
    t_len = m // nb
    nt = t_len // tm
    cos, sa, sb = _rope_tables(pos)
    row = lambda w: pl.BlockSpec((tm, w), lambda i: (i, 0))
    heads = pl.BlockSpec((tm, N_HEADS, QK_DIM), lambda i: (i, 0, 0))
    tab = pl.BlockSpec((tm, LANES), lambda i: (i % nt, 0))
    out_shape = (jax.ShapeDtypeStruct((t_len, nb * SSM_WIDTH), BF16),
                 jax.ShapeDtypeStruct((m, ATTN_WIDTH), BF16),
                 jax.ShapeDtypeStruct((m, N_HEADS, QK_DIM), F32), jax.ShapeDtypeStruct((m, ATTN_WIDTH), BF16),
                 jax.ShapeDtypeStruct((m, N_HEADS, QK_DIM), F32), jax.ShapeDtypeStruct((m, ATTN_WIDTH), BF16),
                 jax.ShapeDtypeStruct((m, D_MODEL), BF16), jax.ShapeDtypeStruct((m, D_MODEL), BF16))
    out_specs = (pl.BlockSpec((tm, SSM_WIDTH), lambda i: (i % nt, i // nt)),
                 row(ATTN_WIDTH), heads, row(ATTN_WIDTH), heads, row(ATTN_WIDTH),
                 row(D_MODEL), row(D_MODEL))
    return pl.pallas_call(
        _in_proj_kernel, grid=(m // tm,),
        in_specs=[row(D_MODEL), _resident((1, D_MODEL)), _resident((D_MODEL, IN_WIDTH)), tab, tab, tab],
        out_specs=out_specs, out_shape=out_shape,
        compiler_params=_params(("parallel",)), name="in_proj")(x2d, g_pre, w_in_bf, cos, sa, sb)


def _ssm_drive(u, bd_ref, store):
    n_tiles = 2 * SSM_LANES // 256
    for j in range(n_tiles):
        ks = LANES * ((j % (n_tiles // 2)) // 2)
        lanes = slice(256 * j, 256 * (j + 1))
        store(lanes, _dot(u[:, ks:ks + LANES], bd_ref[ks:ks + LANES, lanes]))


def _ssm_readout(load, u, cre_ref, cim_ref, dsk_ref):
    half = SSM_LANES // 2
    ys = []
    for j in range(2):
        hre = load(slice(half * j, half * (j + 1))).astype(BF16)
        him = load(slice(SSM_LANES + half * j, SSM_LANES + half * (j + 1))).astype(BF16)
        cols = slice(256 * j, 256 * (j + 1))
        ys.append(_dot(hre, cre_ref[half * j:half * (j + 1), cols]) + _dot(him, cim_ref[half * j:half * (j + 1), cols]))
    y = jnp.concatenate(ys, axis=1) + dsk_ref[...] * u.astype(F32)
    return jax.nn.gelu(y).astype(BF16)


def _glu(yg, wglu_ref):
    return _dot(yg, wglu_ref[:, 0:D_MODEL]) * jax.nn.sigmoid(_dot(yg, wglu_ref[:, D_MODEL:2 * D_MODEL]))


def _ssm_prompt_kernel(u_ref, are_ref, aim_ref, bd_ref, cre_ref, cim_ref, dsk_ref, wglu_ref, perm_ref,
                       ys_ref, hre_ref, him_ref, h_s, st_re, st_im, *, nb, lc):
    @pl.when(pl.program_id(0) == 0)
    def _():
        st_re[...] = jnp.zeros_like(st_re)
        st_im[...] = jnp.zeros_like(st_im)

    rows = lc * nb
    n_tr = rows // SUBLANES
    u = u_ref[...]

    def store(lanes, val):
        h_s[:, :, lanes] = val.reshape(n_tr, SUBLANES, val.shape[-1])

    _ssm_drive(u, bd_ref, store)
    ar = jnp.broadcast_to(are_ref[...], (nb, SSM_LANES))
    ai = jnp.broadcast_to(aim_ref[...], (nb, SSM_LANES))

    def step(s, carry):
        hr, hi = carry
        for k in range(SUBLANES // nb):
            sub = slice(k * nb, (k + 1) * nb)
            nhr = ar * hr - ai * hi + h_s[s, sub, 0:SSM_LANES]
            nhi = ar * hi + ai * hr + h_s[s, sub, SSM_LANES:2 * SSM_LANES]
            h_s[s, sub, 0:SSM_LANES] = nhr
            h_s[s, sub, SSM_LANES:2 * SSM_LANES] = nhi
            hr, hi = nhr, nhi
        return hr, hi

    hr, hi = lax.fori_loop(0, n_tr, step, (st_re[...], st_im[...]))
    st_re[...] = hr
    st_im[...] = hi
    hre_ref[...] = hr
    him_ref[...] = hi

    load = lambda lanes: h_s[:, :, lanes].reshape(rows, lanes.stop - lanes.start)
    yg = _ssm_readout(load, u, cre_ref, cim_ref, dsk_ref)
    ygp = _dot(perm_ref[...], yg).astype(BF16)
    res = _glu(ygp, wglu_ref)
    for b in range(nb):
        ys_ref[b] = res[b * lc:(b + 1) * lc].astype(BF16)


def _ssm_prompt(u_il, nb, t_len, are, aim, bd, cre, cim, dsk, wglu, lc):
    rows = lc * nb
    idx = jnp.arange(rows)
    perm = (idx[None, :] == ((idx % lc) * nb + idx // lc)[:, None]).astype(BF16)
    kern = functools.partial(_ssm_prompt_kernel, nb=nb, lc=lc)
    out_shape = (jax.ShapeDtypeStruct((nb, t_len, D_MODEL), BF16),
                 jax.ShapeDtypeStruct((nb, SSM_LANES), F32), jax.ShapeDtypeStruct((nb, SSM_LANES), F32))
    return pl.pallas_call(
        kern, grid=(t_len // lc,),
        in_specs=[pl.BlockSpec((rows, SSM_WIDTH), lambda c: (c, 0)),
                  _resident((1, SSM_LANES)), _resident((1, SSM_LANES)),
                  _resident((SSM_WIDTH, 2 * SSM_LANES)), _resident((SSM_LANES, SSM_WIDTH)),
                  _resident((SSM_LANES, SSM_WIDTH)), _resident((1, SSM_WIDTH)),
                  _resident((SSM_WIDTH, 2 * D_MODEL)), _resident((rows, rows))],
        out_specs=(pl.BlockSpec((nb, lc, D_MODEL), lambda c: (0, c, 0)),
                   pl.BlockSpec((nb, SSM_LANES), lambda c: (0, 0)), pl.BlockSpec((nb, SSM_LANES), lambda c: (0, 0))),
        out_shape=out_shape,
        scratch_shapes=[pltpu.VMEM((rows // SUBLANES, SUBLANES, 2 * SSM_LANES), F32),
                        pltpu.VMEM((nb, SSM_LANES), F32), pltpu.VMEM((nb, SSM_LANES), F32)],
        compiler_params=_params(("arbitrary",)), name="ssm_prompt")(
            u_il.reshape(t_len * nb, SSM_WIDTH), are, aim, bd, cre, cim, dsk, wglu, perm)


def _ssm_step_kernel(u_ref, h0re_ref, h0im_ref, are_ref, aim_ref, bd_ref, cre_ref, cim_ref, dsk_ref, wglu_ref,
                     ys_ref, hre_ref, him_ref, h_s):
    u = u_ref[...]

    def store(lanes, val):
        h_s[:, lanes] = val

    _ssm_drive(u, bd_ref, store)
    ar, ai = are_ref[...], aim_ref[...]
    hr, hi = h0re_ref[...], h0im_ref[...]
    nhr = ar * hr - ai * hi + h_s[:, 0:SSM_LANES]
    nhi = ar * hi + ai * hr + h_s[:, SSM_LANES:2 * SSM_LANES]
    hre_ref[...] = nhr
    him_ref[...] = nhi
    h_s[:, 0:SSM_LANES] = nhr
    h_s[:, SSM_LANES:2 * SSM_LANES] = nhi
    yg = _ssm_readout(lambda lanes: h_s[:, lanes], u, cre_ref, cim_ref, dsk_ref)
    ys_ref[...] = _glu(yg, wglu_ref).astype(BF16)


def _ssm_step(u, h0re, h0im, are, aim, bd, cre, cim, dsk, wglu):
    n = u.shape[0]
    out_shape = (jax.ShapeDtypeStruct((n, D_MODEL), BF16),
                 jax.ShapeDtypeStruct((n, SSM_LANES), F32), jax.ShapeDtypeStruct((n, SSM_LANES), F32))
    return pl.pallas_call(
        _ssm_step_kernel, out_shape=out_shape,
        scratch_shapes=[pltpu.VMEM((n, 2 * SSM_LANES), F32)],
        compiler_params=pltpu.CompilerParams(vmem_limit_bytes=VMEM_LIMIT), name="ssm_step")(
            u, h0re, h0im, are, aim, bd, cre, cim, dsk, wglu)


def _subln(o1, l1, o2, l2, lam, gain):
    o = o1 / l1 - lam * (o2 / l2)
    return _rms(o, gain) * (1.0 - LAM_INIT)


def _attn_prompt_kernel(lam_ref, q_ref, k_ref, v_ref, gain_ref, o_ref, m_s, l_s, acc_s, *, tq):
    qi = pl.program_id(2)
    q = q_ref[...]
    lane = lax.broadcasted_iota(jnp.int32, (tq, QK_DIM), 1)
    zero = jnp.zeros_like(q)
    qz = jnp.concatenate([jnp.where(lane < HEAD_DIM, q, zero), jnp.where(lane >= HEAD_DIM, q, zero)], axis=0)
    m_s[...] = jnp.full_like(m_s, -jnp.inf)
    l_s[...] = jnp.zeros_like(l_s)
    acc_s[...] = jnp.zeros_like(acc_s)

    def tile(j, diagonal):
        r0 = pl.multiple_of(j * tq, tq)
        k = k_ref[pl.ds(r0, tq), :]
        v = v_ref[pl.ds(r0, tq), :]
        s = _dot_nt(qz, k)
        if diagonal:
            row = lax.broadcasted_iota(jnp.int32, (2 * tq, tq), 0)
            col = lax.broadcasted_iota(jnp.int32, (2 * tq, tq), 1)
            row = jnp.where(row >= tq, row - tq, row)
            s = jnp.where(col <= row, s, -jnp.inf)
        m_prev = m_s[...]
        m_new = jnp.maximum(m_prev, jnp.max(s, axis=1, keepdims=True))
        alpha = jnp.exp(m_prev - m_new)
        p = jnp.exp(s - m_new)
        l_s[...] = alpha * l_s[...] + jnp.sum(p, axis=1, keepdims=True)
        acc_s[...] = alpha * acc_s[...] + _dot(p.astype(BF16), v)
        m_s[...] = m_new

    def body(j, c):
        tile(j, False)
        return c

    lax.fori_loop(0, qi, body, 0)
    tile(qi, True)
    acc, l = acc_s[...], l_s[...]
    o_ref[...] = _subln(acc[0:tq], l[0:tq], acc[tq:2 * tq], l[tq:2 * tq], lam_ref[0, 0], gain_ref[...]).astype(BF16)


def _attn_prompt(lam, q, k, v, gain, nb, t_len, tq):
    kern = functools.partial(_attn_prompt_kernel, tq=tq)
    qspec = pl.BlockSpec((None, tq, QK_DIM), lambda b, h, i: (b, i, h))
    kvspec = pl.BlockSpec((None, t_len, QK_DIM), lambda b, h, i: (b, 0, h))
    return pl.pallas_call(
        kern, grid=(nb, N_HEADS, t_len // tq),
        in_specs=[pl.BlockSpec(memory_space=pltpu.SMEM), qspec, kvspec, kvspec,
                  pl.BlockSpec((1, QK_DIM), lambda b, h, i: (0, 0))],
        out_specs=qspec, out_shape=jax.ShapeDtypeStruct((nb, t_len, ATTN_WIDTH), BF16),
        scratch_shapes=[pltpu.VMEM((2 * tq, 1), F32), pltpu.VMEM((2 * tq, 1), F32), pltpu.VMEM((2 * tq, QK_DIM), F32)],
        compiler_params=_params(("parallel", "parallel", "arbitrary")), name="attn_prompt")(lam, q, k, v, gain)


def _attn_decode_kernel(pt_ref, lam_ref, q_ref, *refs, n_steps, n_pp):
    del pt_ref
    k_refs, v_refs = refs[0:n_pp], refs[n_pp:2 * n_pp]
    kn_ref, vn_ref, gain_ref, o_ref, m_s, l_s, acc_s = refs[2 * n_pp:]
    step = pl.program_id(1)
    n_maps = 2 * N_HEADS
    q = jnp.broadcast_to(q_ref[0].astype(F32), (n_maps, ATTN_WIDTH))
    lane = lax.broadcasted_iota(jnp.int32, (n_maps, ATTN_WIDTH), 1)
    mp = lax.broadcasted_iota(jnp.int32, (n_maps, ATTN_WIDTH), 0)
    keep = (lane // QK_DIM == mp % N_HEADS) & ((lane % QK_DIM) // HEAD_DIM == mp // N_HEADS)
    qz = jnp.where(keep, q, 0.0).astype(BF16)

    @pl.when(step == 0)
    def _():
        m_s[...] = jnp.full_like(m_s, -jnp.inf)
        l_s[...] = jnp.zeros_like(l_s)
        acc_s[...] = jnp.zeros_like(acc_s)

    def flat(ref):
        return jnp.concatenate([ref[:, hh, :] for hh in range(N_HEADS)], axis=1).astype(BF16)

    s = jnp.concatenate([_dot_nt(qz, flat(k_refs[i])) for i in range(n_pp)], axis=1)
    m_prev = m_s[...]
    m_new = jnp.maximum(m_prev, jnp.max(s, axis=1, keepdims=True))
    alpha = jnp.exp(m_prev - m_new)
    p = jnp.exp(s - m_new)
    l_s[...] = alpha * l_s[...] + jnp.sum(p, axis=1, keepdims=True)
    pb = p.astype(BF16)
    pv = _dot(pb[:, 0:PAGE_SIZE], flat(v_refs[0]))
    for i in range(1, n_pp):
        pv = pv + _dot(pb[:, i * PAGE_SIZE:(i + 1) * PAGE_SIZE], flat(v_refs[i]))
    acc_s[...] = alpha * acc_s[...] + pv
    m_s[...] = m_new

    @pl.when(step == n_steps - 1)
    def _():
        kn = kn_ref[0].astype(F32)
        vn = vn_ref[0].astype(F32)
        s_n = jnp.sum(qz.astype(F32) * kn, axis=1, keepdims=True)
        m_prev = m_s[...]
        m_new = jnp.maximum(m_prev, s_n)
        alpha = jnp.exp(m_prev - m_new)
        p_n = jnp.exp(s_n - m_new)
        l = alpha * l_s[...] + p_n
        acc = alpha * acc_s[...] + p_n * vn
        cat = lambda xs: jnp.concatenate(xs, axis=0)
        o1 = cat([acc[hh:hh + 1, hh * QK_DIM:(hh + 1) * QK_DIM] for hh in range(N_HEADS)])
        o2 = cat([acc[N_HEADS + hh:N_HEADS + hh + 1, hh * QK_DIM:(hh + 1) * QK_DIM] for hh in range(N_HEADS)])
        o_ref[0] = _subln(o1, l[0:N_HEADS], o2, l[N_HEADS:n_maps], lam_ref[0, 0], gain_ref[...]).astype(BF16)


def _attn_decode(page_table, lam, q, cache_k, cache_v, k_new, v_new, gain):
    n, n_pages = page_table.shape
    n_pp = DECODE_PAGES_PER_STEP
    n_steps = n_pages // n_pp
    kern = functools.partial(_attn_decode_kernel, n_steps=n_steps, n_pp=n_pp)
    n_maps = 2 * N_HEADS
    row = pl.BlockSpec((1, 1, ATTN_WIDTH), lambda b, s, pt: (b, 0, 0))

    def page(i):
        return pl.BlockSpec((None, PAGE_SIZE, N_HEADS, QK_DIM), lambda b, s, pt: (pt[b, s * n_pp + i], 0, 0, 0))

    pages = [page(i) for i in range(n_pp)]
    grid_spec = pltpu.PrefetchScalarGridSpec(
        num_scalar_prefetch=1, grid=(n, n_steps),
        in_specs=[pl.BlockSpec(memory_space=pltpu.SMEM), row] + pages + pages + [
            row, row, pl.BlockSpec((1, QK_DIM), lambda b, s, pt: (0, 0))],
        out_specs=pl.BlockSpec((1, N_HEADS, QK_DIM), lambda b, s, pt: (b, 0, 0)),
        scratch_shapes=[pltpu.VMEM((n_maps, 1), F32), pltpu.VMEM((n_maps, 1), F32),
                        pltpu.VMEM((n_maps, ATTN_WIDTH), F32)])
    r3 = lambda a: a.reshape(n, 1, ATTN_WIDTH)
    return pl.pallas_call(
        kern, grid_spec=grid_spec, out_shape=jax.ShapeDtypeStruct((n, N_HEADS, QK_DIM), BF16),
        compiler_params=_params(("parallel", "arbitrary")), name="attn_decode")(
            page_table, lam, r3(q), *([cache_k] * n_pp), *([cache_v] * n_pp), r3(k_new), r3(v_new), gain)


FF_CHUNKS = ((0, 1024), (1024, 2048), (2048, D_FF))


def _out_ffn_kernel(x_ref, ys_ref, o_ref, gs_ref, ga_ref, wo_ref, npm_ref, npf_ref, wg_ref, wu_ref, wd_ref, nof_ref,
                    y_ref):
    mixed = gs_ref[...].astype(F32) * ys_ref[...].astype(F32) + ga_ref[...].astype(F32) * o_ref[...].astype(F32)
    x1 = x_ref[...] + _rms(_dot(mixed.astype(BF16), wo_ref[...]), npm_ref[...])
    hf = _rms(x1, npf_ref[...]).astype(BF16)
    f = None
    for c0, c1 in FF_CHUNKS:
        act = (jax.nn.silu(_dot(hf, wg_ref[:, c0:c1])) * _dot(hf, wu_ref[:, c0:c1])).astype(BF16)
        part = _dot(act, wd_ref[c0:c1, :])
        f = part if f is None else f + part
    y_ref[...] = x1 + _rms(f, nof_ref[...])


def _out_ffn(x2d, ys, o, gs, ga, wo, npm, npf, wg, wu, wd, nof, tm):
    m = x2d.shape[0]
    row = pl.BlockSpec((tm, D_MODEL), lambda i: (i, 0))
    vec = _resident((1, D_MODEL))
    return pl.pallas_call(
        _out_ffn_kernel, grid=(m // tm,),
        in_specs=[row, row, row, row, row, _resident((D_MODEL, D_MODEL)), vec, vec,
                  _resident((D_MODEL, D_FF)), _resident((D_MODEL, D_FF)), _resident((D_FF, D_MODEL)), vec],
        out_specs=row, out_shape=jax.ShapeDtypeStruct((m, D_MODEL), F32),
        compiler_params=_params(("parallel",)), name="out_ffn")(x2d, ys, o, gs, ga, wo, npm, npf, wg, wu, wd, nof)


def kernel(x_prompt, x_sample, cache_k, cache_v, state_ssm_re, state_ssm_im, page_table, norm_pre_mix, w_in,
           ssm_lambda_re, ssm_lambda_im, ssm_log_dt, ssm_b_re, ssm_b_im, ssm_c_re, ssm_c_im, ssm_d, w_glu_a,
           w_glu_b, lambda_q1, lambda_k1, lambda_q2, lambda_k2, subln_gain, w_o, norm_post_mix, norm_pre_ffn,
           w_gate, w_up, w_down, norm_post_ffn):
    assert w_in.shape[0] == 1, "single-layer stack"
    bp, tp, _ = x_prompt.shape
    bs, ts, _ = x_sample.shape
    assert ts == 1

    are, aim, bbre, bbim, lam = _prep(ssm_lambda_re[0], ssm_lambda_im[0], ssm_log_dt[0], ssm_b_re[0], ssm_b_im[0],
                                      lambda_q1[0], lambda_k1[0], lambda_q2[0], lambda_k2[0])
    are, aim = are.reshape(1, SSM_LANES), aim.reshape(1, SSM_LANES)
    bd = jnp.concatenate([_block_diag(bbre), _block_diag(bbim)], axis=1).astype(BF16)
    cre = _block_diag(jnp.swapaxes(ssm_c_re[0], 1, 2)).astype(BF16)
    cim = _block_diag(jnp.swapaxes(-ssm_c_im[0], 1, 2)).astype(BF16)
    dsk = ssm_d[0].reshape(1, SSM_WIDTH)
    wglu = jnp.concatenate([w_glu_a[0], w_glu_b[0]], axis=1).astype(BF16)
    vec = lambda a: a[0].reshape(1, -1)
    w_in_bf, wo_bf = w_in[0].astype(BF16), w_o[0].astype(BF16)
    wg_bf, wu_bf, wd_bf = w_gate[0].astype(BF16), w_up[0].astype(BF16), w_down[0].astype(BF16)
    gain = vec(subln_gain)
    ssm_w = (are, aim, bd, cre, cim, dsk, wglu)
    ffn_w = (wo_bf, vec(norm_post_mix), vec(norm_pre_ffn), wg_bf, wu_bf, wd_bf, vec(norm_post_ffn))

    xp = x_prompt.reshape(bp * tp, D_MODEL)
    u_il, q, kf, kb, vf, vb, gs, ga = _in_proj(xp, jnp.arange(tp), bp, vec(norm_pre_mix), w_in_bf, tm=512)
    ys, hpr, hpi = _ssm_prompt(u_il, bp, tp, *ssm_w, lc=128)
    r3 = lambda a: a.reshape(bp, tp, ATTN_WIDTH)
    o = _attn_prompt(lam, r3(q), r3(kb), r3(vb), gain, bp, tp, tq=256)
    yp = _out_ffn(xp, ys.reshape(bp * tp, D_MODEL), o.reshape(bp * tp, ATTN_WIDTH), gs, ga, *ffn_w, tm=512)

    xs = x_sample.reshape(bs, D_MODEL)
    pos_s = jnp.full((bs,), PAST_LEN, jnp.int32)
    u_s, q_s, kf_s, kb_s, vf_s, vb_s, gs_s, ga_s = _in_proj(xs, pos_s, 1, vec(norm_pre_mix), w_in_bf, tm=bs)
    ys_s, hsr, hsi = _ssm_step(u_s, state_ssm_re[0].reshape(bs, SSM_LANES), state_ssm_im[0].reshape(bs, SSM_LANES),
                               *ssm_w)
    o_s = _attn_decode(page_table, lam, q_s, cache_k[0], cache_v[0], kb_s, vb_s, gain)
    ysamp = _out_ffn(xs, ys_s, o_s.reshape(bs, ATTN_WIDTH), gs_s, ga_s, *ffn_w, tm=bs)

    st = lambda a, n: a.reshape(1, n, SSM_GROUPS, SSM_STATE)
    kv = lambda a, n, t: a.reshape(1, n, t, N_HEADS, QK_DIM)
    return (yp.reshape(bp, tp, D_MODEL), ysamp.reshape(bs, ts, D_MODEL),
            kv(kf, bp, tp), kv(vf, bp, tp), st(hpr, bp), st(hpi, bp),
            kv(kf_s, bs, ts), kv(vf_s, bs, ts), st(hsr, bs), st(hsi, bs))
```

```python
import functools
import math

import jax
import jax.numpy as jnp
from jax import lax
from jax.experimental import pallas as pl
from jax.experimental.pallas import tpu as pltpu

F32 = jnp.float32
BF16 = jnp.bfloat16

D_MODEL = 1024
PAST_LEN = 2048
PAGE_SIZE = 128
SSM_WIDTH = 512
SSM_GROUP = 16
SSM_GROUPS = 32
SSM_STATE = 64
SSM_LANES = SSM_GROUPS * SSM_STATE
HEAD_DIM = 64
N_HEADS = 8
QK_DIM = 128
ATTN_WIDTH = 1024
ROPE_DIM = 16
ROPE_THETA = 500000.0
D_FF = 2816
IN_WIDTH = SSM_WIDTH + 3 * ATTN_WIDTH + 2 * D_MODEL
EPS = 1e-6
LAM_INIT = 0.8 - 0.6 * math.exp(-0.3 * 0)
Q_SCALE = HEAD_DIM ** -0.5 * math.log2(math.e)

LANES = 128
SUBLANES = 8
VMEM_LIMIT = 56 * 1024 * 1024
DECODE_PAGES_PER_STEP = 8


def _params(sem):
    return pltpu.CompilerParams(dimension_semantics=sem, vmem_limit_bytes=VMEM_LIMIT)


def _resident(shape):
    return pl.BlockSpec(shape, lambda *_: (0,) * len(shape), pipeline_mode=pl.Buffered(1))


def _rms(x, g):
    return x * lax.rsqrt(jnp.mean(x * x, axis=-1, keepdims=True) + EPS) * g


def _dot(a, b):
    return jnp.dot(a, b, preferred_element_type=F32)


def _dot_nt(a, b):
    return lax.dot_general(a, b, (((1,), (1,)), ((), ())), preferred_element_type=F32)


def _prep_kernel(lre_ref, lim_ref, ldt_ref, bre_ref, bim_ref, lq1_ref, lk1_ref, lq2_ref, lk2_ref,
                 are_ref, aim_ref, bbre_ref, bbim_ref, lam_ref):
    lre, lim = lre_ref[...], lim_ref[...]
    dt = jnp.exp(ldt_ref[...])
    mag = jnp.exp(lre * dt)
    ar, ai = mag * jnp.cos(lim * dt), mag * jnp.sin(lim * dt)
    den = lre * lre + lim * lim
    fr = ((ar - 1.0) * lre + ai * lim) / den
    fi = (ai * lre - (ar - 1.0) * lim) / den
    are_ref[...] = ar
    aim_ref[...] = ai
    bre, bim = bre_ref[...], bim_ref[...]
    bbre_ref[...] = fr[:, None, :] * bre - fi[:, None, :] * bim
    bbim_ref[...] = fr[:, None, :] * bim + fi[:, None, :] * bre
    s1 = jnp.sum(lq1_ref[...] * lk1_ref[...], axis=-1, keepdims=True)
    s2 = jnp.sum(lq2_ref[...] * lk2_ref[...], axis=-1, keepdims=True)
    lam_ref[...] = jnp.exp(s1) - jnp.exp(s2) + LAM_INIT


def _prep(lam_re, lam_im, log_dt, b_re, b_im, lq1, lk1, lq2, lk2):
    g, p, c = SSM_GROUPS, SSM_STATE, SSM_GROUP
    out_shape = (jax.ShapeDtypeStruct((g, p), F32), jax.ShapeDtypeStruct((g, p), F32),
                 jax.ShapeDtypeStruct((g, c, p), F32), jax.ShapeDtypeStruct((g, c, p), F32),
                 jax.ShapeDtypeStruct((1, 1), F32))
    return pl.pallas_call(_prep_kernel, out_shape=out_shape, name="ssm_prep")(
        lam_re, lam_im, log_dt.reshape(g, 1), jnp.swapaxes(b_re, 1, 2), jnp.swapaxes(b_im, 1, 2),
        lq1.reshape(1, HEAD_DIM), lk1.reshape(1, HEAD_DIM), lq2.reshape(1, HEAD_DIM), lk2.reshape(1, HEAD_DIM))


def _block_diag(blocks):
    g, r, c = blocks.shape
    eye = jnp.eye(g, dtype=blocks.dtype)
    return (blocks[:, :, None, :] * eye[:, None, :, None]).reshape(g * r, g * c)


def _in_proj_kernel(x_ref, g_ref, w_ref, cos_ref, sa_ref, sb_ref,
                    u_ref, q_ref, kf_ref, kb_ref, vf_ref, vb_ref, gs_ref, ga_ref):
    h = _rms(x_ref[...], g_ref[...]).astype(BF16)
    o0 = SSM_WIDTH
    o1, o2, o3, o4 = o0 + ATTN_WIDTH, o0 + 2 * ATTN_WIDTH, o0 + 3 * ATTN_WIDTH, o0 + 3 * ATTN_WIDTH + D_MODEL

    u_ref[...] = _dot(h, w_ref[:, 0:o0]).astype(BF16)
    cos, sa, sb = cos_ref[...], sa_ref[...], sb_ref[...]

    def rope(z):
        return z * cos + pltpu.roll(z, LANES - ROPE_DIM // 2, 1) * sa + pltpu.roll(z, ROPE_DIM // 2, 1) * sb

    tm = x_ref.shape[0]
    zq = _dot(h, w_ref[:, o0:o1])
    for hh in range(N_HEADS):
        sl = slice(hh * QK_DIM, (hh + 1) * QK_DIM)
        q_ref[:, sl] = (rope(zq[:, sl]) * Q_SCALE).astype(BF16)
    zk = _dot(h, w_ref[:, o1:o2])
    for hh in range(N_HEADS):
        sl = slice(hh * QK_DIM, (hh + 1) * QK_DIM)
        kr = rope(zk[:, sl])
        kf_ref[pl.ds(hh, tm, stride=N_HEADS), :] = kr
        kb_ref[:, sl] = kr.astype(BF16)
    zv = _dot(h, w_ref[:, o2:o3])
    for hh in range(N_HEADS):
        vf_ref[pl.ds(hh, tm, stride=N_HEADS), :] = zv[:, hh * QK_DIM:(hh + 1) * QK_DIM]
    vb_ref[...] = zv.astype(BF16)
    gs_ref[...] = jax.nn.sigmoid(_dot(h, w_ref[:, o3:o4])).astype(BF16)
    ga_ref[...] = jax.nn.sigmoid(_dot(h, w_ref[:, o4:IN_WIDTH])).astype(BF16)


def _rope_tables(pos):
    half = ROPE_DIM // 2
    inv = ROPE_THETA ** (-jnp.arange(half, dtype=F32) * 2.0 / ROPE_DIM)
    ang = pos.astype(F32)[:, None] * inv[None, :]
    cos, sin = jnp.cos(ang), jnp.sin(ang)
    n = pos.shape[0]
    comp_cos = jnp.concatenate([cos, cos, jnp.ones((n, HEAD_DIM - ROPE_DIM), F32)], axis=1)
    comp_sa = jnp.concatenate([-sin, jnp.zeros((n, HEAD_DIM - half), F32)], axis=1)
    comp_sb = jnp.concatenate([jnp.zeros((n, half), F32), sin, jnp.zeros((n, HEAD_DIM - ROPE_DIM), F32)], axis=1)
    two = lambda t: jnp.concatenate([t, t], axis=1)
    return two(comp_cos), two(comp_sa), two(comp_sb)


def _in_proj(x2d, pos, nb, g_pre, w_in_bf, tm):
    m = x2d.shape[0]
    t_len = m // nb
    nt = t_len // tm
    cos, sa, sb = _rope_tables(pos)
    row = lambda w: pl.BlockSpec((tm, w), lambda i: (i, 0))
    heads = pl.BlockSpec((tm * N_HEADS, QK_DIM), lambda i: (i, 0))
    tab = pl.BlockSpec((tm, LANES), lambda i: (i % nt, 0))
    out_shape = (jax.ShapeDtypeStruct((t_len, nb * SSM_WIDTH), BF16),
                 jax.ShapeDtypeStruct((m, ATTN_WIDTH), BF16),
                 jax.ShapeDtypeStruct((m * N_HEADS, QK_DIM), F32), jax.ShapeDtypeStruct((m, ATTN_WIDTH), BF16),
                 jax.ShapeDtypeStruct((m * N_HEADS, QK_DIM), F32), jax.ShapeDtypeStruct((m, ATTN_WIDTH), BF16),
                 jax.ShapeDtypeStruct((m, D_MODEL), BF16), jax.ShapeDtypeStruct((m, D_MODEL), BF16))
    out_specs = (pl.BlockSpec((tm, SSM_WIDTH), lambda i: (i % nt, i // nt)),
                 row(ATTN_WIDTH), heads, row(ATTN_WIDTH), heads, row(ATTN_WIDTH),
                 row(D_MODEL), row(D_MODEL))
    return pl.pallas_call(
        _in_proj_kernel, grid=(m // tm,),
        in_specs=[row(D_MODEL), _resident((1, D_MODEL)), _resident((D_MODEL, IN_WIDTH)), tab, tab, tab],
        out_specs=out_specs, out_shape=out_shape,
        compiler_params=_params(("parallel",)), name="in_proj")(x2d, g_pre, w_in_bf, cos, sa, sb)


def _ssm_drive(u, bd_ref, store):
    n_tiles = 2 * SSM_LANES // 256
    for j in range(n_tiles):
        ks = LANES * ((j % (n_tiles // 2)) // 2)
        lanes = slice(256 * j, 256 * (j + 1))
        store(lanes, _dot(u[:, ks:ks + LANES], bd_ref[ks:ks + LANES, lanes]))


def _ssm_readout(load, u, cre_ref, cim_ref, dsk_ref):
    half = SSM_LANES // 2
    ys = []
    for j in range(2):
        hre = load(slice(half * j, half * (j + 1))).astype(BF16)
        him = load(slice(SSM_LANES + half * j, SSM_LANES + half * (j + 1))).astype(BF16)
        cols = slice(256 * j, 256 * (j + 1))
        ys.append(_dot(hre, cre_ref[half * j:half * (j + 1), cols]) + _dot(him, cim_ref[half * j:half * (j + 1), cols]))
    y = jnp.concatenate(ys, axis=1) + dsk_ref[...] * u.astype(F32)
    return jax.nn.gelu(y).astype(BF16)


def _glu(yg, wglu_ref):
    return _dot(yg, wglu_ref[:, 0:D_MODEL]) * jax.nn.sigmoid(_dot(yg, wglu_ref[:, D_MODEL:2 * D_MODEL]))


def _ssm_prompt_kernel(u_ref, are_ref, aim_ref, bd_ref, cre_ref, cim_ref, dsk_ref, wglu_ref, perm_ref,
                       ys_ref, hre_ref, him_ref, h_s, st_re, st_im, *, nb, lc):
    @pl.when(pl.program_id(0) == 0)
    def _():
        st_re[...] = jnp.zeros_like(st_re)
        st_im[...] = jnp.zeros_like(st_im)

    rows = lc * nb
    n_tr = rows // SUBLANES
    u = u_ref[...]

    def store(lanes, val):
        h_s[:, :, lanes] = val.reshape(n_tr, SUBLANES, val.shape[-1])

    _ssm_drive(u, bd_ref, store)
    ar = jnp.broadcast_to(are_ref[...], (nb, SSM_LANES))
    ai = jnp.broadcast_to(aim_ref[...], (nb, SSM_LANES))

    def step(s, carry):
        hr, hi = carry
        for k in range(SUBLANES // nb):
            sub = slice(k * nb, (k + 1) * nb)
            nhr = ar * hr - ai * hi + h_s[s, sub, 0:SSM_LANES]
            nhi = ar * hi + ai * hr + h_s[s, sub, SSM_LANES:2 * SSM_LANES]
            h_s[s, sub, 0:SSM_LANES] = nhr
            h_s[s, sub, SSM_LANES:2 * SSM_LANES] = nhi
            hr, hi = nhr, nhi
        return hr, hi

    hr, hi = lax.fori_loop(0, n_tr, step, (st_re[...], st_im[...]))
    st_re[...] = hr
    st_im[...] = hi
    hre_ref[...] = hr
    him_ref[...] = hi

    load = lambda lanes: h_s[:, :, lanes].reshape(rows, lanes.stop - lanes.start)
    yg = _ssm_readout(load, u, cre_ref, cim_ref, dsk_ref)
    ygp = _dot(perm_ref[...], yg).astype(BF16)
    res = _glu(ygp, wglu_ref)
    for b in range(nb):
        ys_ref[b] = res[b * lc:(b + 1) * lc].astype(BF16)


def _ssm_prompt(u_il, nb, t_len, are, aim, bd, cre, cim, dsk, wglu, lc):
    rows = lc * nb
    idx = jnp.arange(rows)
    perm = (idx[None, :] == ((idx % lc) * nb + idx // lc)[:, None]).astype(BF16)
    kern = functools.partial(_ssm_prompt_kernel, nb=nb, lc=lc)
    out_shape = (jax.ShapeDtypeStruct((nb, t_len, D_MODEL), BF16),
                 jax.ShapeDtypeStruct((nb, SSM_LANES), F32), jax.ShapeDtypeStruct((nb, SSM_LANES), F32))
    return pl.pallas_call(
        kern, grid=(t_len // lc,),
        in_specs=[pl.BlockSpec((rows, SSM_WIDTH), lambda c: (c, 0)),
                  _resident((1, SSM_LANES)), _resident((1, SSM_LANES)),
                  _resident((SSM_WIDTH, 2 * SSM_LANES)), _resident((SSM_LANES, SSM_WIDTH)),
                  _resident((SSM_LANES, SSM_WIDTH)), _resident((1, SSM_WIDTH)),
                  _resident((SSM_WIDTH, 2 * D_MODEL)), _resident((rows, rows))],
        out_specs=(pl.BlockSpec((nb, lc, D_MODEL), lambda c: (0, c, 0)),
                   pl.BlockSpec((nb, SSM_LANES), lambda c: (0, 0)), pl.BlockSpec((nb, SSM_LANES), lambda c: (0, 0))),
        out_shape=out_shape,
        scratch_shapes=[pltpu.VMEM((rows // SUBLANES, SUBLANES, 2 * SSM_LANES), F32),
                        pltpu.VMEM((nb, SSM_LANES), F32), pltpu.VMEM((nb, SSM_LANES), F32)],
        compiler_params=_params(("arbitrary",)), name="ssm_prompt")(
            u_il.reshape(t_len * nb, SSM_WIDTH), are, aim, bd, cre, cim, dsk, wglu, perm)


def _ssm_step_kernel(u_ref, h0re_ref, h0im_ref, are_ref, aim_ref, bd_ref, cre_ref, cim_ref, dsk_ref, wglu_ref,
                     ys_ref, hre_ref, him_ref, h_s):
    u = u_ref[...]

    def store(lanes, val):
        h_s[:, lanes] = val

    _ssm_drive(u, bd_ref, store)
    ar, ai = are_ref[...], aim_ref[...]
    hr, hi = h0re_ref[...], h0im_ref[...]
    nhr = ar * hr - ai * hi + h_s[:, 0:SSM_LANES]
    nhi = ar * hi + ai * hr + h_s[:, SSM_LANES:2 * SSM_LANES]
    hre_ref[...] = nhr
    him_ref[...] = nhi
    h_s[:, 0:SSM_LANES] = nhr
    h_s[:, SSM_LANES:2 * SSM_LANES] = nhi
    yg = _ssm_readout(lambda lanes: h_s[:, lanes], u, cre_ref, cim_ref, dsk_ref)
    ys_ref[...] = _glu(yg, wglu_ref).astype(BF16)


def _ssm_step(u, h0re, h0im, are, aim, bd, cre, cim, dsk, wglu):
    n = u.shape[0]
    out_shape = (jax.ShapeDtypeStruct((n, D_MODEL), BF16),
                 jax.ShapeDtypeStruct((n, SSM_LANES), F32), jax.ShapeDtypeStruct((n, SSM_LANES), F32))
    return pl.pallas_call(
        _ssm_step_kernel, out_shape=out_shape,
        scratch_shapes=[pltpu.VMEM((n, 2 * SSM_LANES), F32)],
        compiler_params=pltpu.CompilerParams(vmem_limit_bytes=VMEM_LIMIT), name="ssm_step")(
            u, h0re, h0im, are, aim, bd, cre, cim, dsk, wglu)


def _subln(o1, l1, o2, l2, lam, gain):
    o = o1 / l1 - lam * (o2 / l2)
    return _rms(o, gain) * (1.0 - LAM_INIT)


def _attn_prompt_kernel(lam_ref, q_ref, k_ref, v_ref, gain_ref, o_ref, sa_s, sb_s, m_s, acc_s, *, tq, tk):
    qi = pl.program_id(2)
    q = q_ref[...]
    lane = lax.broadcasted_iota(jnp.int32, (tq, QK_DIM), 1)
    zero = jnp.zeros_like(q)
    qz = jnp.concatenate([jnp.where(lane < HEAD_DIM, q, zero), jnp.where(lane >= HEAD_DIM, q, zero)], axis=0)
    m_s[...] = jnp.full_like(m_s, -jnp.inf)
    acc_s[...] = jnp.zeros_like(acc_s)
    ones = jnp.ones((tk, LANES), BF16)
    wide = lambda a: jnp.concatenate([a] * (tk // LANES), axis=1)

    def scores(c, s_ref):
        r0 = pl.multiple_of(c * tk, tk)
        s_ref[...] = _dot_nt(qz, k_ref[pl.ds(r0, tk), :])

    def softmax_pv(c, s_ref, mask_shift):
        s = s_ref[...]
        if mask_shift is not None:
            row = lax.broadcasted_iota(jnp.int32, (2 * tq, tk), 0)
            col = lax.broadcasted_iota(jnp.int32, (2 * tq, tk), 1)
            row = jnp.where(row >= tq, row - tq, row)
            s = jnp.where(col + mask_shift <= row, s, -jnp.inf)
        m_prev = m_s[...]
        m_new = jnp.maximum(m_prev, jnp.max(s, axis=1, keepdims=True))
        alpha = jnp.exp2(m_prev - m_new)
        p = jnp.exp2(s - wide(m_new)).astype(BF16)
        r0 = pl.multiple_of(c * tk, tk)
        v1 = jnp.concatenate([v_ref[pl.ds(r0, tk), :], ones], axis=1)
        acc_s[...] = jnp.concatenate([alpha, alpha], axis=1) * acc_s[...] + _dot(p, v1)
        m_s[...] = m_new

    scores(0, sa_s)

    def body(j, carry):
        c0 = 2 * j
        scores(c0 + 1, sb_s)
        softmax_pv(c0, sa_s, None)
        scores(c0 + 2, sa_s)
        softmax_pv(c0 + 1, sb_s, None)
        return carry

    lax.fori_loop(0, qi, body, 0)
    c0 = 2 * qi
    scores(c0 + 1, sb_s)
    softmax_pv(c0, sa_s, 0)
    softmax_pv(c0 + 1, sb_s, tk)
    acc = acc_s[...]
    o_ref[...] = _subln(acc[0:tq, 0:QK_DIM], acc[0:tq, QK_DIM:2 * QK_DIM], acc[tq:2 * tq, 0:QK_DIM],
                        acc[tq:2 * tq, QK_DIM:2 * QK_DIM], lam_ref[0, 0], gain_ref[...]).astype(BF16)


def _attn_prompt(lam, q, k, v, gain, nb, t_len, tq):
    tk = tq // 2
    kern = functools.partial(_attn_prompt_kernel, tq=tq, tk=tk)
    qspec = pl.BlockSpec((None, tq, QK_DIM), lambda b, h, i: (b, i, h))
    kvspec = pl.BlockSpec((None, t_len, QK_DIM), lambda b, h, i: (b, 0, h))
    return pl.pallas_call(
        kern, grid=(nb, N_HEADS, t_len // tq),
        in_specs=[pl.BlockSpec(memory_space=pltpu.SMEM), qspec, kvspec, kvspec,
                  pl.BlockSpec((1, QK_DIM), lambda b, h, i: (0, 0))],
        out_specs=qspec, out_shape=jax.ShapeDtypeStruct((nb, t_len, ATTN_WIDTH), BF16),
        scratch_shapes=[pltpu.VMEM((2 * tq, tk), F32), pltpu.VMEM((2 * tq, tk), F32),
                        pltpu.VMEM((2 * tq, LANES), F32), pltpu.VMEM((2 * tq, 2 * QK_DIM), F32)],
        compiler_params=_params(("parallel", "parallel", "arbitrary")), name="attn_prompt")(lam, q, k, v, gain)


def _attn_decode_kernel(pt_ref, lam_ref, q_ref, *refs, n_steps, n_pp):
    del pt_ref
    k_refs, v_refs = refs[0:n_pp], refs[n_pp:2 * n_pp]
    kn_ref, vn_ref, gain_ref, o_ref, m_s, l_s, acc_s = refs[2 * n_pp:]
    step = pl.program_id(1)
    n_maps = 2 * N_HEADS
    q = jnp.broadcast_to(q_ref[0].astype(F32), (n_maps, ATTN_WIDTH))
    lane = lax.broadcasted_iota(jnp.int32, (n_maps, ATTN_WIDTH), 1)
    mp = lax.broadcasted_iota(jnp.int32, (n_maps, ATTN_WIDTH), 0)
    keep = (lane // QK_DIM == mp % N_HEADS) & ((lane % QK_DIM) // HEAD_DIM == mp // N_HEADS)
    qz = jnp.where(keep, q, 0.0).astype(BF16)

    @pl.when(step == 0)
    def _():
        m_s[...] = jnp.full_like(m_s, -jnp.inf)
        l_s[...] = jnp.zeros_like(l_s)
        acc_s[...] = jnp.zeros_like(acc_s)

    def flat(ref):
        heads = [ref[pl.ds(hh, PAGE_SIZE, stride=N_HEADS), :] for hh in range(N_HEADS)]
        return jnp.concatenate(heads, axis=1).astype(BF16)

    s = jnp.concatenate([_dot_nt(qz, flat(k_refs[i])) for i in range(n_pp)], axis=1)
    m_prev = m_s[...]
    m_new = jnp.maximum(m_prev, jnp.max(s, axis=1, keepdims=True))
    alpha = jnp.exp2(m_prev - m_new)
    p = jnp.exp2(s - m_new)
    l_s[...] = alpha * l_s[...] + jnp.sum(p, axis=1, keepdims=True)
    pb = p.astype(BF16)
    pv = _dot(pb[:, 0:PAGE_SIZE], flat(v_refs[0]))
    for i in range(1, n_pp):
        pv = pv + _dot(pb[:, i * PAGE_SIZE:(i + 1) * PAGE_SIZE], flat(v_refs[i]))
    acc_s[...] = alpha * acc_s[...] + pv
    m_s[...] = m_new

    @pl.when(step == n_steps - 1)
    def _():
        kn = kn_ref[0].astype(F32)
        vn = vn_ref[0].astype(F32)
        s_n = jnp.sum(qz.astype(F32) * kn, axis=1, keepdims=True)
        m_prev = m_s[...]
        m_new = jnp.maximum(m_prev, s_n)
        alpha = jnp.exp2(m_prev - m_new)
        p_n = jnp.exp2(s_n - m_new)
        l = alpha * l_s[...] + p_n
        acc = alpha * acc_s[...] + p_n * vn
        cat = lambda xs: jnp.concatenate(xs, axis=0)
        o1 = cat([acc[hh:hh + 1, hh * QK_DIM:(hh + 1) * QK_DIM] for hh in range(N_HEADS)])
        o2 = cat([acc[N_HEADS + hh:N_HEADS + hh + 1, hh * QK_DIM:(hh + 1) * QK_DIM] for hh in range(N_HEADS)])
        o_ref[0] = _subln(o1, l[0:N_HEADS], o2, l[N_HEADS:n_maps], lam_ref[0, 0], gain_ref[...]).astype(BF16)


def _attn_decode(page_table, lam, q, cache_k, cache_v, k_new, v_new, gain):
    n, n_pages = page_table.shape
    n_pp = DECODE_PAGES_PER_STEP
    n_steps = n_pages // n_pp
    kern = functools.partial(_attn_decode_kernel, n_steps=n_steps, n_pp=n_pp)
    n_maps = 2 * N_HEADS
    row = pl.BlockSpec((1, 1, ATTN_WIDTH), lambda b, s, pt: (b, 0, 0))

    def page(i):
        return pl.BlockSpec((None, PAGE_SIZE * N_HEADS, QK_DIM), lambda b, s, pt: (pt[b, s * n_pp + i], 0, 0))

    pages = [page(i) for i in range(n_pp)]
    grid_spec = pltpu.PrefetchScalarGridSpec(
        num_scalar_prefetch=1, grid=(n, n_steps),
        in_specs=[pl.BlockSpec(memory_space=pltpu.SMEM), row] + pages + pages + [
            row, row, pl.BlockSpec((1, QK_DIM), lambda b, s, pt: (0, 0))],
        out_specs=pl.BlockSpec((1, N_HEADS, QK_DIM), lambda b, s, pt: (b, 0, 0)),
        scratch_shapes=[pltpu.VMEM((n_maps, 1), F32), pltpu.VMEM((n_maps, 1), F32),
                        pltpu.VMEM((n_maps, ATTN_WIDTH), F32)])
    r3 = lambda a: a.reshape(n, 1, ATTN_WIDTH)
    return pl.pallas_call(
        kern, grid_spec=grid_spec, out_shape=jax.ShapeDtypeStruct((n, N_HEADS, QK_DIM), BF16),
        compiler_params=_params(("parallel", "arbitrary")), name="attn_decode")(
            page_table, lam, r3(q), *([cache_k] * n_pp), *([cache_v] * n_pp), r3(k_new), r3(v_new), gain)


FF_CHUNKS = ((0, 1024), (1024, 2048), (2048, D_FF))


def _out_ffn_kernel(x_ref, ys_ref, o_ref, gs_ref, ga_ref, wo_ref, npm_ref, npf_ref, wg_ref, wu_ref, wd_ref, nof_ref,
                    y_ref):
    mixed = gs_ref[...].astype(F32) * ys_ref[...].astype(F32) + ga_ref[...].astype(F32) * o_ref[...].astype(F32)
    x1 = x_ref[...] + _rms(_dot(mixed.astype(BF16), wo_ref[...]), npm_ref[...])
    hf = _rms(x1, npf_ref[...]).astype(BF16)
    f = None
    for c0, c1 in FF_CHUNKS:
        act = (jax.nn.silu(_dot(hf, wg_ref[:, c0:c1])) * _dot(hf, wu_ref[:, c0:c1])).astype(BF16)
        part = _dot(act, wd_ref[c0:c1, :])
        f = part if f is None else f + part
    y_ref[...] = x1 + _rms(f, nof_ref[...])


def _out_ffn(x2d, ys, o, gs, ga, wo, npm, npf, wg, wu, wd, nof, tm):
    m = x2d.shape[0]
    row = pl.BlockSpec((tm, D_MODEL), lambda i: (i, 0))
    vec = _resident((1, D_MODEL))
    return pl.pallas_call(
        _out_ffn_kernel, grid=(m // tm,),
        in_specs=[row, row, row, row, row, _resident((D_MODEL, D_MODEL)), vec, vec,
                  _resident((D_MODEL, D_FF)), _resident((D_MODEL, D_FF)), _resident((D_FF, D_MODEL)), vec],
        out_specs=row, out_shape=jax.ShapeDtypeStruct((m, D_MODEL), F32),
        compiler_params=_params(("parallel",)), name="out_ffn")(x2d, ys, o, gs, ga, wo, npm, npf, wg, wu, wd, nof)


def kernel(x_prompt, x_sample, cache_k, cache_v, state_ssm_re, state_ssm_im, page_table, norm_pre_mix, w_in,
           ssm_lambda_re, ssm_lambda_im, ssm_log_dt, ssm_b_re, ssm_b_im, ssm_c_re, ssm_c_im, ssm_d, w_glu_a,
           w_glu_b, lambda_q1, lambda_k1, lambda_q2, lambda_k2, subln_gain, w_o, norm_post_mix, norm_pre_ffn,
           w_gate, w_up, w_down, norm_post_ffn):
    assert w_in.shape[0] == 1, "single-layer stack"
    bp, tp, _ = x_prompt.shape
    bs, ts, _ = x_sample.shape
    assert ts == 1

    are, aim, bbre, bbim, lam = _prep(ssm_lambda_re[0], ssm_lambda_im[0], ssm_log_dt[0], ssm_b_re[0], ssm_b_im[0],
                                      lambda_q1[0], lambda_k1[0], lambda_q2[0], lambda_k2[0])
    are, aim = are.reshape(1, SSM_LANES), aim.reshape(1, SSM_LANES)
    bd = jnp.concatenate([_block_diag(bbre), _block_diag(bbim)], axis=1).astype(BF16)
    cre = _block_diag(jnp.swapaxes(ssm_c_re[0], 1, 2)).astype(BF16)
    cim = _block_diag(jnp.swapaxes(-ssm_c_im[0], 1, 2)).astype(BF16)
    dsk = ssm_d[0].reshape(1, SSM_WIDTH)
    wglu = jnp.concatenate([w_glu_a[0], w_glu_b[0]], axis=1).astype(BF16)
    vec = lambda a: a[0].reshape(1, -1)
    w_in_bf, wo_bf = w_in[0].astype(BF16), w_o[0].astype(BF16)
    wg_bf, wu_bf, wd_bf = w_gate[0].astype(BF16), w_up[0].astype(BF16), w_down[0].astype(BF16)
    gain = vec(subln_gain)
    ssm_w = (are, aim, bd, cre, cim, dsk, wglu)
    ffn_w = (wo_bf, vec(norm_post_mix), vec(norm_pre_ffn), wg_bf, wu_bf, wd_bf, vec(norm_post_ffn))

    xp = x_prompt.reshape(bp * tp, D_MODEL)
    u_il, q, kf, kb, vf, vb, gs, ga = _in_proj(xp, jnp.arange(tp), bp, vec(norm_pre_mix), w_in_bf, tm=512)
    ys, hpr, hpi = _ssm_prompt(u_il, bp, tp, *ssm_w, lc=128)
    r3 = lambda a: a.reshape(bp, tp, ATTN_WIDTH)
    o = _attn_prompt(lam, r3(q), r3(kb), r3(vb), gain, bp, tp, tq=512)
    yp = _out_ffn(xp, ys.reshape(bp * tp, D_MODEL), o.reshape(bp * tp, ATTN_WIDTH), gs, ga, *ffn_w, tm=512)

    xs = x_sample.reshape(bs, D_MODEL)
    pos_s = jnp.full((bs,), PAST_LEN, jnp.int32)
    u_s, q_s, kf_s, kb_s, vf_s, vb_s, gs_s, ga_s = _in_proj(xs, pos_s, 1, vec(norm_pre_mix), w_in_bf, tm=bs)
    ys_s, hsr, hsi = _ssm_step(u_s, state_ssm_re[0].reshape(bs, SSM_LANES), state_ssm_im[0].reshape(bs, SSM_LANES),
                               *ssm_w)
    n_phys = cache_k.shape[1]
    pages = lambda c: c[0].reshape(n_phys, PAGE_SIZE * N_HEADS, QK_DIM)
    o_s = _attn_decode(page_table, lam, q_s, pages(cache_k), pages(cache_v), kb_s, vb_s, gain)
    ysamp = _out_ffn(xs, ys_s, o_s.reshape(bs, ATTN_WIDTH), gs_s, ga_s, *ffn_w, tm=bs)

    st = lambda a, n: a.reshape(1, n, SSM_GROUPS, SSM_STATE)
    kv = lambda a, n, t: a.reshape(1, n, t, N_HEADS, QK_DIM)
    return (yp.reshape(bp, tp, D_MODEL), ysamp.reshape(bs, ts, D_MODEL),
            kv(kf, bp, tp), kv(vf, bp, tp), st(hpr, bp), st(hpi, bp),
            kv(kf_s, bs, ts), kv(vf_s, bs, ts), st(hsr, bs), st(hsi, bs))
```

```python
import functools
import math

import jax
import jax.numpy as jnp
from jax import lax
from jax.experimental import pallas as pl
from jax.experimental.pallas import tpu as pltpu

F32 = jnp.float32
BF16 = jnp.bfloat16

D_MODEL = 1024
PAST_LEN = 2048
PAGE_SIZE = 128
SSM_WIDTH = 512
SSM_GROUP = 16
SSM_GROUPS = 32
SSM_STATE = 64
SSM_LANES = SSM_GROUPS * SSM_STATE
HEAD_DIM = 64
N_HEADS = 8
QK_DIM = 128
ATTN_WIDTH = 1024
ROPE_DIM = 16
ROPE_THETA = 500000.0
D_FF = 2816
IN_WIDTH = SSM_WIDTH + 3 * ATTN_WIDTH + 2 * D_MODEL
EPS = 1e-6
LAM_INIT = 0.8 - 0.6 * math.exp(-0.3 * 0)
Q_SCALE = HEAD_DIM ** -0.5 * math.log2(math.e)

LANES = 128
SUBLANES = 8
VMEM_LIMIT = 56 * 1024 * 1024
DECODE_PAGES_PER_STEP = 8


def _params(sem):
    return pltpu.CompilerParams(dimension_semantics=sem, vmem_limit_bytes=VMEM_LIMIT)


def _resident(shape):
    return pl.BlockSpec(shape, lambda *_: (0,) * len(shape), pipeline_mode=pl.Buffered(1))


def _rms(x, g):
    return x * lax.rsqrt(jnp.mean(x * x, axis=-1, keepdims=True) + EPS) * g


def _dot(a, b):
    return jnp.dot(a, b, preferred_element_type=F32)


def _dot_nt(a, b):
    return lax.dot_general(a, b, (((1,), (1,)), ((), ())), preferred_element_type=F32)


def _prep_kernel(lre_ref, lim_ref, ldt_ref, bre_ref, bim_ref, lq1_ref, lk1_ref, lq2_ref, lk2_ref,
                 are_ref, aim_ref, bbre_ref, bbim_ref, lam_ref):
    lre, lim = lre_ref[...], lim_ref[...]
    dt = jnp.exp(ldt_ref[...])
    mag = jnp.exp(lre * dt)
    ar, ai = mag * jnp.cos(lim * dt), mag * jnp.sin(lim * dt)
    den = lre * lre + lim * lim
    fr = ((ar - 1.0) * lre + ai * lim) / den
    fi = (ai * lre - (ar - 1.0) * lim) / den
    are_ref[...] = ar
    aim_ref[...] = ai
    bre, bim = bre_ref[...], bim_ref[...]
    bbre_ref[...] = fr[:, None, :] * bre - fi[:, None, :] * bim
    bbim_ref[...] = fr[:, None, :] * bim + fi[:, None, :] * bre
    s1 = jnp.sum(lq1_ref[...] * lk1_ref[...], axis=-1, keepdims=True)
    s2 = jnp.sum(lq2_ref[...] * lk2_ref[...], axis=-1, keepdims=True)
    lam_ref[...] = jnp.exp(s1) - jnp.exp(s2) + LAM_INIT


def _prep(lam_re, lam_im, log_dt, b_re, b_im, lq1, lk1, lq2, lk2):
    g, p, c = SSM_GROUPS, SSM_STATE, SSM_GROUP
    out_shape = (jax.ShapeDtypeStruct((g, p), F32), jax.ShapeDtypeStruct((g, p), F32),
                 jax.ShapeDtypeStruct((g, c, p), F32), jax.ShapeDtypeStruct((g, c, p), F32),
                 jax.ShapeDtypeStruct((1, 1), F32))
    return pl.pallas_call(_prep_kernel, out_shape=out_shape, name="ssm_prep")(
        lam_re, lam_im, log_dt.reshape(g, 1), jnp.swapaxes(b_re, 1, 2), jnp.swapaxes(b_im, 1, 2),
        lq1.reshape(1, HEAD_DIM), lk1.reshape(1, HEAD_DIM), lq2.reshape(1, HEAD_DIM), lk2.reshape(1, HEAD_DIM))


def _block_diag(blocks):
    g, r, c = blocks.shape
    eye = jnp.eye(g, dtype=blocks.dtype)
    return (blocks[:, :, None, :] * eye[:, None, :, None]).reshape(g * r, g * c)


def _in_proj_kernel(x_ref, g_ref, w_ref, cos_ref, sa_ref, sb_ref,
                    u_ref, q_ref, kf_ref, kb_ref, vf_ref, vb_ref, gs_ref, ga_ref):
    h = _rms(x_ref[...], g_ref[...]).astype(BF16)
    o0 = SSM_WIDTH
    o1, o2, o3, o4 = o0 + ATTN_WIDTH, o0 + 2 * ATTN_WIDTH, o0 + 3 * ATTN_WIDTH, o0 + 3 * ATTN_WIDTH + D_MODEL

    u_ref[...] = _dot(h, w_ref[:, 0:o0]).astype(BF16)
    cos, sa, sb = cos_ref[...], sa_ref[...], sb_ref[...]

    def rope(z):
        return z * cos + pltpu.roll(z, LANES - ROPE_DIM // 2, 1) * sa + pltpu.roll(z, ROPE_DIM // 2, 1) * sb

    tm = x_ref.shape[0]
    zq = _dot(h, w_ref[:, o0:o1])
    for hh in range(N_HEADS):
        sl = slice(hh * QK_DIM, (hh + 1) * QK_DIM)
        q_ref[:, sl] = (rope(zq[:, sl]) * Q_SCALE).astype(BF16)
    zk = _dot(h, w_ref[:, o1:o2])
    for hh in range(N_HEADS):
        sl = slice(hh * QK_DIM, (hh + 1) * QK_DIM)
        kr = rope(zk[:, sl])
        kf_ref[pl.ds(hh, tm, stride=N_HEADS), :] = kr
        kb_ref[:, sl] = kr.astype(BF16)
    zv = _dot(h, w_ref[:, o2:o3])
    for hh in range(N_HEADS):
        vf_ref[pl.ds(hh, tm, stride=N_HEADS), :] = zv[:, hh * QK_DIM:(hh + 1) * QK_DIM]
    vb_ref[...] = zv.astype(BF16)
    gs_ref[...] = jax.nn.sigmoid(_dot(h, w_ref[:, o3:o4])).astype(BF16)
    ga_ref[...] = jax.nn.sigmoid(_dot(h, w_ref[:, o4:IN_WIDTH])).astype(BF16)


def _rope_tables(pos):
    half = ROPE_DIM // 2
    inv = ROPE_THETA ** (-jnp.arange(half, dtype=F32) * 2.0 / ROPE_DIM)
    ang = pos.astype(F32)[:, None] * inv[None, :]
    cos, sin = jnp.cos(ang), jnp.sin(ang)
    n = pos.shape[0]
    comp_cos = jnp.concatenate([cos, cos, jnp.ones((n, HEAD_DIM - ROPE_DIM), F32)], axis=1)
    comp_sa = jnp.concatenate([-sin, jnp.zeros((n, HEAD_DIM - half), F32)], axis=1)
    comp_sb = jnp.concatenate([jnp.zeros((n, half), F32), sin, jnp.zeros((n, HEAD_DIM - ROPE_DIM), F32)], axis=1)
    two = lambda t: jnp.concatenate([t, t], axis=1)
    return two(comp_cos), two(comp_sa), two(comp_sb)


def _in_proj(x2d, pos, nb, g_pre, w_in_bf, tm):
    m = x2d.shape[0]
    t_len = m // nb
    nt = t_len // tm
    cos, sa, sb = _rope_tables(pos)
    row = lambda w: pl.BlockSpec((tm, w), lambda i: (i, 0))
    heads = pl.BlockSpec((tm * N_HEADS, QK_DIM), lambda i: (i, 0))
    tab = pl.BlockSpec((tm, LANES), lambda i: (i % nt, 0))
    out_shape = (jax.ShapeDtypeStruct((t_len, nb * SSM_WIDTH), BF16),
                 jax.ShapeDtypeStruct((m, ATTN_WIDTH), BF16),
                 jax.ShapeDtypeStruct((m * N_HEADS, QK_DIM), F32), jax.ShapeDtypeStruct((m, ATTN_WIDTH), BF16),
                 jax.ShapeDtypeStruct((m * N_HEADS, QK_DIM), F32), jax.ShapeDtypeStruct((m, ATTN_WIDTH), BF16),
                 jax.ShapeDtypeStruct((m, D_MODEL), BF16), jax.ShapeDtypeStruct((m, D_MODEL), BF16))
    out_specs = (pl.BlockSpec((tm, SSM_WIDTH), lambda i: (i % nt, i // nt)),
                 row(ATTN_WIDTH), heads, row(ATTN_WIDTH), heads, row(ATTN_WIDTH),
                 row(D_MODEL), row(D_MODEL))
    return pl.pallas_call(
        _in_proj_kernel, grid=(m // tm,),
        in_specs=[row(D_MODEL), _resident((1, D_MODEL)), _resident((D_MODEL, IN_WIDTH)), tab, tab, tab],
        out_specs=out_specs, out_shape=out_shape,
        compiler_params=_params(("parallel",)), name="in_proj")(x2d, g_pre, w_in_bf, cos, sa, sb)


def _ssm_drive(u, bd_ref, store):
    n_tiles = 2 * SSM_LANES // 256
    for j in range(n_tiles):
        ks = LANES * ((j % (n_tiles // 2)) // 2)
        lanes = slice(256 * j, 256 * (j + 1))
        store(lanes, _dot(u[:, ks:ks + LANES], bd_ref[ks:ks + LANES, lanes]))


def _ssm_readout(load, u, cre_ref, cim_ref, dsk_ref):
    half = SSM_LANES // 2
    ys = []
    for j in range(2):
        hre = load(slice(half * j, half * (j + 1))).astype(BF16)
        him = load(slice(SSM_LANES + half * j, SSM_LANES + half * (j + 1))).astype(BF16)
        cols = slice(256 * j, 256 * (j + 1))
        ys.append(_dot(hre, cre_ref[half * j:half * (j + 1), cols]) + _dot(him, cim_ref[half * j:half * (j + 1), cols]))
    y = jnp.concatenate(ys, axis=1) + dsk_ref[...] * u.astype(F32)
    return jax.nn.gelu(y).astype(BF16)


def _glu(yg, wglu_ref):
    return _dot(yg, wglu_ref[:, 0:D_MODEL]) * jax.nn.sigmoid(_dot(yg, wglu_ref[:, D_MODEL:2 * D_MODEL]))


def _ssm_prompt_kernel(u_ref, are_ref, aim_ref, bd_ref, cre_ref, cim_ref, dsk_ref, wglu_ref, perm_ref, permt_ref,
                       ys_ref, hre_ref, him_ref, h_s, st_re, st_im, *, nb, lc):
    @pl.when(pl.program_id(0) == 0)
    def _():
        st_re[...] = jnp.zeros_like(st_re)
        st_im[...] = jnp.zeros_like(st_im)

    rows = lc * nb
    n_tr = rows // SUBLANES
    u_bt = jnp.concatenate([u_ref[:, b * SSM_WIDTH:(b + 1) * SSM_WIDTH] for b in range(nb)], axis=0)
    u = _dot(permt_ref[...], u_bt).astype(BF16)

    def store(lanes, val):
        h_s[:, :, lanes] = val.reshape(n_tr, SUBLANES, val.shape[-1])

    _ssm_drive(u, bd_ref, store)
    ar = jnp.broadcast_to(are_ref[...], (nb, SSM_LANES))
    ai = jnp.broadcast_to(aim_ref[...], (nb, SSM_LANES))

    def step(s, carry):
        hr, hi = carry
        for k in range(SUBLANES // nb):
            sub = slice(k * nb, (k + 1) * nb)
            nhr = ar * hr - ai * hi + h_s[s, sub, 0:SSM_LANES]
            nhi = ar * hi + ai * hr + h_s[s, sub, SSM_LANES:2 * SSM_LANES]
            h_s[s, sub, 0:SSM_LANES] = nhr
            h_s[s, sub, SSM_LANES:2 * SSM_LANES] = nhi
            hr, hi = nhr, nhi
        return hr, hi

    hr, hi = lax.fori_loop(0, n_tr, step, (st_re[...], st_im[...]))
    st_re[...] = hr
    st_im[...] = hi
    hre_ref[...] = hr
    him_ref[...] = hi

    load = lambda lanes: h_s[:, :, lanes].reshape(rows, lanes.stop - lanes.start)
    yg = _ssm_readout(load, u, cre_ref, cim_ref, dsk_ref)
    ygp = _dot(perm_ref[...], yg).astype(BF16)
    res = _glu(ygp, wglu_ref)
    for b in range(nb):
        ys_ref[b] = res[b * lc:(b + 1) * lc].astype(BF16)


def _ssm_prompt(u_il, nb, t_len, are, aim, bd, cre, cim, dsk, wglu, lc):
    rows = lc * nb
    idx = jnp.arange(rows)
    perm = (idx[None, :] == ((idx % lc) * nb + idx // lc)[:, None]).astype(BF16)
    kern = functools.partial(_ssm_prompt_kernel, nb=nb, lc=lc)
    out_shape = (jax.ShapeDtypeStruct((nb, t_len, D_MODEL), BF16),
                 jax.ShapeDtypeStruct((nb, SSM_LANES), F32), jax.ShapeDtypeStruct((nb, SSM_LANES), F32))
    return pl.pallas_call(
        kern, grid=(t_len // lc,),
        in_specs=[pl.BlockSpec((lc, nb * SSM_WIDTH), lambda c: (c, 0)),
                  _resident((1, SSM_LANES)), _resident((1, SSM_LANES)),
                  _resident((SSM_WIDTH, 2 * SSM_LANES)), _resident((SSM_LANES, SSM_WIDTH)),
                  _resident((SSM_LANES, SSM_WIDTH)), _resident((1, SSM_WIDTH)),
                  _resident((SSM_WIDTH, 2 * D_MODEL)), _resident((rows, rows)), _resident((rows, rows))],
        out_specs=(pl.BlockSpec((nb, lc, D_MODEL), lambda c: (0, c, 0)),
                   pl.BlockSpec((nb, SSM_LANES), lambda c: (0, 0)), pl.BlockSpec((nb, SSM_LANES), lambda c: (0, 0))),
        out_shape=out_shape,
        scratch_shapes=[pltpu.VMEM((rows // SUBLANES, SUBLANES, 2 * SSM_LANES), F32),
                        pltpu.VMEM((nb, SSM_LANES), F32), pltpu.VMEM((nb, SSM_LANES), F32)],
        compiler_params=_params(("arbitrary",)), name="ssm_prompt")(
            u_il, are, aim, bd, cre, cim, dsk, wglu, perm, perm.T)


def _ssm_step_kernel(u_ref, h0re_ref, h0im_ref, are_ref, aim_ref, bd_ref, cre_ref, cim_ref, dsk_ref, wglu_ref,
                     ys_ref, hre_ref, him_ref, h_s):
    u = u_ref[...]

    def store(lanes, val):
        h_s[:, lanes] = val

    _ssm_drive(u, bd_ref, store)
    ar, ai = are_ref[...], aim_ref[...]
    hr, hi = h0re_ref[...], h0im_ref[...]
    nhr = ar * hr - ai * hi + h_s[:, 0:SSM_LANES]
    nhi = ar * hi + ai * hr + h_s[:, SSM_LANES:2 * SSM_LANES]
    hre_ref[...] = nhr
    him_ref[...] = nhi
    h_s[:, 0:SSM_LANES] = nhr
    h_s[:, SSM_LANES:2 * SSM_LANES] = nhi
    yg = _ssm_readout(lambda lanes: h_s[:, lanes], u, cre_ref, cim_ref, dsk_ref)
    ys_ref[...] = _glu(yg, wglu_ref).astype(BF16)


def _ssm_step(u, h0re, h0im, are, aim, bd, cre, cim, dsk, wglu):
    n = u.shape[0]
    out_shape = (jax.ShapeDtypeStruct((n, D_MODEL), BF16),
                 jax.ShapeDtypeStruct((n, SSM_LANES), F32), jax.ShapeDtypeStruct((n, SSM_LANES), F32))
    return pl.pallas_call(
        _ssm_step_kernel, out_shape=out_shape,
        scratch_shapes=[pltpu.VMEM((n, 2 * SSM_LANES), F32)],
        compiler_params=pltpu.CompilerParams(vmem_limit_bytes=VMEM_LIMIT), name="ssm_step")(
            u, h0re, h0im, are, aim, bd, cre, cim, dsk, wglu)


def _subln(o1, l1, o2, l2, lam, gain):
    o = o1 / l1 - lam * (o2 / l2)
    return _rms(o, gain) * (1.0 - LAM_INIT)


def _attn_prompt_kernel(lam_ref, q_ref, k_ref, v_ref, gain_ref, o_ref, sa_s, sb_s, m_s, acc_s, *, tq, tk):
    qi = pl.program_id(2)
    lane = lax.broadcasted_iota(jnp.int32, (tk, QK_DIM), 1)
    parts = []
    for half in range(2):
        q = q_ref[half * tk:(half + 1) * tk, :]
        zero = jnp.zeros_like(q)
        parts += [jnp.where(lane < HEAD_DIM, q, zero), jnp.where(lane >= HEAD_DIM, q, zero)]
    qz = jnp.concatenate(parts, axis=0)
    m_s[...] = jnp.full_like(m_s, -jnp.inf)
    acc_s[...] = jnp.zeros_like(acc_s)
    ones = jnp.ones((tk, LANES), BF16)
    wide = lambda a: jnp.concatenate([a] * (tk // LANES), axis=1)

    def scores(c, s_ref, rows=slice(0, 2 * tq)):
        r0 = pl.multiple_of(c * tk, tk)
        s_ref[rows, :] = _dot_nt(qz[rows], k_ref[pl.ds(r0, tk), :])

    def softmax_pv(c, s_ref, rows=slice(0, 2 * tq), diag_rows=0):
        s = s_ref[rows, :]
        if diag_rows:
            row = lax.broadcasted_iota(jnp.int32, (diag_rows, tk), 0) % tk
            col = lax.broadcasted_iota(jnp.int32, (diag_rows, tk), 1)
            masked = jnp.where(col <= row, s[0:diag_rows], -jnp.inf)
            s = masked if diag_rows == s.shape[0] else jnp.concatenate([masked, s[diag_rows:]], axis=0)
        m_prev = m_s[rows, :]
        m_new = jnp.maximum(m_prev, jnp.max(s, axis=1, keepdims=True))
        alpha = jnp.exp2(m_prev - m_new)
        p = jnp.exp2(s - wide(m_new)).astype(BF16)
        r0 = pl.multiple_of(c * tk, tk)
        v1 = jnp.concatenate([v_ref[pl.ds(r0, tk), :], ones], axis=1)
        acc_s[rows, :] = jnp.concatenate([alpha, alpha], axis=1) * acc_s[rows, :] + _dot(p, v1)
        m_s[rows, :] = m_new

    scores(0, sa_s)

    def body(j, carry):
        c0 = 2 * j
        scores(c0 + 1, sb_s)
        softmax_pv(c0, sa_s)
        scores(c0 + 2, sa_s)
        softmax_pv(c0 + 1, sb_s)
        return carry

    lax.fori_loop(0, qi, body, 0)
    c0 = 2 * qi
    late = slice(tq, 2 * tq)
    scores(c0 + 1, sb_s, late)
    softmax_pv(c0, sa_s, diag_rows=tq)
    softmax_pv(c0 + 1, sb_s, late, diag_rows=tq)
    acc = acc_s[...]
    lam, gain = lam_ref[0, 0], gain_ref[...]
    for half in range(2):
        a1 = acc[2 * half * tk:(2 * half + 1) * tk]
        a2 = acc[(2 * half + 1) * tk:(2 * half + 2) * tk]
        o_ref[half * tk:(half + 1) * tk, :] = _subln(a1[:, 0:QK_DIM], a1[:, QK_DIM:2 * QK_DIM], a2[:, 0:QK_DIM],
                                                     a2[:, QK_DIM:2 * QK_DIM], lam, gain).astype(BF16)


def _attn_prompt(lam, q, k, v, gain, nb, t_len, tq):
    tk = tq // 2
    kern = functools.partial(_attn_prompt_kernel, tq=tq, tk=tk)
    qspec = pl.BlockSpec((None, tq, QK_DIM), lambda b, h, i: (b, i, h))
    kvspec = pl.BlockSpec((None, t_len, QK_DIM), lambda b, h, i: (b, 0, h))
    return pl.pallas_call(
        kern, grid=(nb, N_HEADS, t_len // tq),
        in_specs=[pl.BlockSpec(memory_space=pltpu.SMEM), qspec, kvspec, kvspec,
                  pl.BlockSpec((1, QK_DIM), lambda b, h, i: (0, 0))],
        out_specs=qspec, out_shape=jax.ShapeDtypeStruct((nb, t_len, ATTN_WIDTH), BF16),
        scratch_shapes=[pltpu.VMEM((2 * tq, tk), F32), pltpu.VMEM((2 * tq, tk), F32),
                        pltpu.VMEM((2 * tq, LANES), F32), pltpu.VMEM((2 * tq, 2 * QK_DIM), F32)],
        compiler_params=_params(("parallel", "parallel", "arbitrary")), name="attn_prompt")(lam, q, k, v, gain)


def _attn_decode_kernel(pt_ref, lam_ref, q_ref, *refs, n_steps, n_pp):
    del pt_ref
    k_refs, v_refs = refs[0:n_pp], refs[n_pp:2 * n_pp]
    kn_ref, vn_ref, gain_ref, o_ref, m_s, l_s, acc_s = refs[2 * n_pp:]
    step = pl.program_id(1)
    n_maps = 2 * N_HEADS
    q = jnp.broadcast_to(q_ref[0].astype(F32), (n_maps, ATTN_WIDTH))
    lane = lax.broadcasted_iota(jnp.int32, (n_maps, ATTN_WIDTH), 1)
    mp = lax.broadcasted_iota(jnp.int32, (n_maps, ATTN_WIDTH), 0)
    keep = (lane // QK_DIM == mp % N_HEADS) & ((lane % QK_DIM) // HEAD_DIM == mp // N_HEADS)
    qz = jnp.where(keep, q, 0.0).astype(BF16)

    @pl.when(step == 0)
    def _():
        m_s[...] = jnp.full_like(m_s, -jnp.inf)
        l_s[...] = jnp.zeros_like(l_s)
        acc_s[...] = jnp.zeros_like(acc_s)

    def flat(ref):
        heads = [ref[pl.ds(hh, PAGE_SIZE, stride=N_HEADS), :] for hh in range(N_HEADS)]
        return jnp.concatenate(heads, axis=1).astype(BF16)

    s = jnp.concatenate([_dot_nt(qz, flat(k_refs[i])) for i in range(n_pp)], axis=1)
    m_prev = m_s[...]
    m_new = jnp.maximum(m_prev, jnp.max(s, axis=1, keepdims=True))
    alpha = jnp.exp2(m_prev - m_new)
    p = jnp.exp2(s - m_new)
    l_s[...] = alpha * l_s[...] + jnp.sum(p, axis=1, keepdims=True)
    pb = p.astype(BF16)
    pv = _dot(pb[:, 0:PAGE_SIZE], flat(v_refs[0]))
    for i in range(1, n_pp):
        pv = pv + _dot(pb[:, i * PAGE_SIZE:(i + 1) * PAGE_SIZE], flat(v_refs[i]))
    acc_s[...] = alpha * acc_s[...] + pv
    m_s[...] = m_new

    @pl.when(step == n_steps - 1)
    def _():
        kn = kn_ref[0].astype(F32)
        vn = vn_ref[0].astype(F32)
        s_n = jnp.sum(qz.astype(F32) * kn, axis=1, keepdims=True)
        m_prev = m_s[...]
        m_new = jnp.maximum(m_prev, s_n)
        alpha = jnp.exp2(m_prev - m_new)
        p_n = jnp.exp2(s_n - m_new)
        l = alpha * l_s[...] + p_n
        acc = alpha * acc_s[...] + p_n * vn
        cat = lambda xs: jnp.concatenate(xs, axis=0)
        o1 = cat([acc[hh:hh + 1, hh * QK_DIM:(hh + 1) * QK_DIM] for hh in range(N_HEADS)])
        o2 = cat([acc[N_HEADS + hh:N_HEADS + hh + 1, hh * QK_DIM:(hh + 1) * QK_DIM] for hh in range(N_HEADS)])
        o_ref[0] = _subln(o1, l[0:N_HEADS], o2, l[N_HEADS:n_maps], lam_ref[0, 0], gain_ref[...]).astype(BF16)


def _attn_decode(page_table, lam, q, cache_k, cache_v, k_new, v_new, gain):
    n, n_pages = page_table.shape
    n_pp = DECODE_PAGES_PER_STEP
    n_steps = n_pages // n_pp
    kern = functools.partial(_attn_decode_kernel, n_steps=n_steps, n_pp=n_pp)
    n_maps = 2 * N_HEADS
    row = pl.BlockSpec((1, 1, ATTN_WIDTH), lambda b, s, pt: (b, 0, 0))

    def page(i):
        return pl.BlockSpec((None, PAGE_SIZE * N_HEADS, QK_DIM), lambda b, s, pt: (pt[b, s * n_pp + i], 0, 0))

    pages = [page(i) for i in range(n_pp)]
    grid_spec = pltpu.PrefetchScalarGridSpec(
        num_scalar_prefetch=1, grid=(n, n_steps),
        in_specs=[pl.BlockSpec(memory_space=pltpu.SMEM), row] + pages + pages + [
            row, row, pl.BlockSpec((1, QK_DIM), lambda b, s, pt: (0, 0))],
        out_specs=pl.BlockSpec((1, N_HEADS, QK_DIM), lambda b, s, pt: (b, 0, 0)),
        scratch_shapes=[pltpu.VMEM((n_maps, 1), F32), pltpu.VMEM((n_maps, 1), F32),
                        pltpu.VMEM((n_maps, ATTN_WIDTH), F32)])
    r3 = lambda a: a.reshape(n, 1, ATTN_WIDTH)
    return pl.pallas_call(
        kern, grid_spec=grid_spec, out_shape=jax.ShapeDtypeStruct((n, N_HEADS, QK_DIM), BF16),
        compiler_params=_params(("parallel", "arbitrary")), name="attn_decode")(
            page_table, lam, r3(q), *([cache_k] * n_pp), *([cache_v] * n_pp), r3(k_new), r3(v_new), gain)


FF_CHUNKS = ((0, 1024), (1024, 2048), (2048, D_FF))


def _out_ffn_kernel(x_ref, ys_ref, o_ref, gs_ref, ga_ref, wo_ref, npm_ref, npf_ref, wg_ref, wu_ref, wd_ref, nof_ref,
                    y_ref):
    mixed = gs_ref[...].astype(F32) * ys_ref[...].astype(F32) + ga_ref[...].astype(F32) * o_ref[...].astype(F32)
    x1 = x_ref[...] + _rms(_dot(mixed.astype(BF16), wo_ref[...]), npm_ref[...])
    hf = _rms(x1, npf_ref[...]).astype(BF16)
    f = None
    for c0, c1 in FF_CHUNKS:
        act = (jax.nn.silu(_dot(hf, wg_ref[:, c0:c1])) * _dot(hf, wu_ref[:, c0:c1])).astype(BF16)
        part = _dot(act, wd_ref[c0:c1, :])
        f = part if f is None else f + part
    y_ref[...] = x1 + _rms(f, nof_ref[...])


def _out_ffn(x2d, ys, o, gs, ga, wo, npm, npf, wg, wu, wd, nof, tm):
    m = x2d.shape[0]
    row = pl.BlockSpec((tm, D_MODEL), lambda i: (i, 0))
    vec = _resident((1, D_MODEL))
    return pl.pallas_call(
        _out_ffn_kernel, grid=(m // tm,),
        in_specs=[row, row, row, row, row, _resident((D_MODEL, D_MODEL)), vec, vec,
                  _resident((D_MODEL, D_FF)), _resident((D_MODEL, D_FF)), _resident((D_FF, D_MODEL)), vec],
        out_specs=row, out_shape=jax.ShapeDtypeStruct((m, D_MODEL), F32),
        compiler_params=_params(("parallel",)), name="out_ffn")(x2d, ys, o, gs, ga, wo, npm, npf, wg, wu, wd, nof)


def kernel(x_prompt, x_sample, cache_k, cache_v, state_ssm_re, state_ssm_im, page_table, norm_pre_mix, w_in,
           ssm_lambda_re, ssm_lambda_im, ssm_log_dt, ssm_b_re, ssm_b_im, ssm_c_re, ssm_c_im, ssm_d, w_glu_a,
           w_glu_b, lambda_q1, lambda_k1, lambda_q2, lambda_k2, subln_gain, w_o, norm_post_mix, norm_pre_ffn,
           w_gate, w_up, w_down, norm_post_ffn):
    assert w_in.shape[0] == 1, "single-layer stack"
    bp, tp, _ = x_prompt.shape
    bs, ts, _ = x_sample.shape
    assert ts == 1

    are, aim, bbre, bbim, lam = _prep(ssm_lambda_re[0], ssm_lambda_im[0], ssm_log_dt[0], ssm_b_re[0], ssm_b_im[0],
                                      lambda_q1[0], lambda_k1[0], lambda_q2[0], lambda_k2[0])
    are, aim = are.reshape(1, SSM_LANES), aim.reshape(1, SSM_LANES)
    bd = jnp.concatenate([_block_diag(bbre), _block_diag(bbim)], axis=1).astype(BF16)
    cre = _block_diag(jnp.swapaxes(ssm_c_re[0], 1, 2)).astype(BF16)
    cim = _block_diag(jnp.swapaxes(-ssm_c_im[0], 1, 2)).astype(BF16)
    dsk = ssm_d[0].reshape(1, SSM_WIDTH)
    wglu = jnp.concatenate([w_glu_a[0], w_glu_b[0]], axis=1).astype(BF16)
    vec = lambda a: a[0].reshape(1, -1)
    w_in_bf, wo_bf = w_in[0].astype(BF16), w_o[0].astype(BF16)
    wg_bf, wu_bf, wd_bf = w_gate[0].astype(BF16), w_up[0].astype(BF16), w_down[0].astype(BF16)
    gain = vec(subln_gain)
    ssm_w = (are, aim, bd, cre, cim, dsk, wglu)
    ffn_w = (wo_bf, vec(norm_post_mix), vec(norm_pre_ffn), wg_bf, wu_bf, wd_bf, vec(norm_post_ffn))

    xp = x_prompt.reshape(bp * tp, D_MODEL)
    u_il, q, kf, kb, vf, vb, gs, ga = _in_proj(xp, jnp.arange(tp), bp, vec(norm_pre_mix), w_in_bf, tm=512)
    ys, hpr, hpi = _ssm_prompt(u_il, bp, tp, *ssm_w, lc=128)
    r3 = lambda a: a.reshape(bp, tp, ATTN_WIDTH)
    o = _attn_prompt(lam, r3(q), r3(kb), r3(vb), gain, bp, tp, tq=1024)
    yp = _out_ffn(xp, ys.reshape(bp * tp, D_MODEL), o.reshape(bp * tp, ATTN_WIDTH), gs, ga, *ffn_w, tm=512)

    xs = x_sample.reshape(bs, D_MODEL)
    pos_s = jnp.full((bs,), PAST_LEN, jnp.int32)
    u_s, q_s, kf_s, kb_s, vf_s, vb_s, gs_s, ga_s = _in_proj(xs, pos_s, 1, vec(norm_pre_mix), w_in_bf, tm=bs)
    ys_s, hsr, hsi = _ssm_step(u_s, state_ssm_re[0].reshape(bs, SSM_LANES), state_ssm_im[0].reshape(bs, SSM_LANES),
                               *ssm_w)
    n_phys = cache_k.shape[1]
    pages = lambda c: c[0].reshape(n_phys, PAGE_SIZE * N_HEADS, QK_DIM)
    o_s = _attn_decode(page_table, lam, q_s, pages(cache_k), pages(cache_v), kb_s, vb_s, gain)
    ysamp = _out_ffn(xs, ys_s, o_s.reshape(bs, ATTN_WIDTH), gs_s, ga_s, *ffn_w, tm=bs)

    st = lambda a, n: a.reshape(1, n, SSM_GROUPS, SSM_STATE)
    kv = lambda a, n, t: a.reshape(1, n, t, N_HEADS, QK_DIM)
    return (yp.reshape(bp, tp, D_MODEL), ysamp.reshape(bs, ts, D_MODEL),
            kv(kf, bp, tp), kv(vf, bp, tp), st(hpr, bp), st(hpi, bp),
            kv(kf_s, bs, ts), kv(vf_s, bs, ts), st(hsr, bs), st(hsi, bs))
```

```python
import functools
import math

import jax
import jax.numpy as jnp
import numpy as np
from jax import lax
from jax.experimental import pallas as pl
from jax.experimental.pallas import tpu as pltpu

F32 = jnp.float32
BF16 = jnp.bfloat16

D_MODEL = 1024
PAST_LEN = 2048
PAGE_SIZE = 128
SSM_WIDTH = 512
SSM_GROUP = 16
SSM_GROUPS = 32
SSM_STATE = 64
SSM_LANES = SSM_GROUPS * SSM_STATE
HEAD_DIM = 64
N_HEADS = 8
QK_DIM = 128
ATTN_WIDTH = 1024
ROPE_DIM = 16
ROPE_THETA = 500000.0
D_FF = 2816
IN_WIDTH = SSM_WIDTH + 3 * ATTN_WIDTH + 2 * D_MODEL
EPS = 1e-6
LAM_INIT = 0.8 - 0.6 * math.exp(-0.3 * 0)
Q_SCALE = HEAD_DIM ** -0.5 * math.log2(math.e)

LANES = 128
SUBLANES = 8
VMEM_LIMIT = 56 * 1024 * 1024
ATTN_VMEM_LIMIT = 62 * 1024 * 1024
DECODE_GROUPS = 2


def _params(sem, vmem=VMEM_LIMIT):
    return pltpu.CompilerParams(dimension_semantics=sem, vmem_limit_bytes=vmem)


def _resident(shape):
    return pl.BlockSpec(shape, lambda *_: (0,) * len(shape), pipeline_mode=pl.Buffered(1))


def _rms(x, g):
    return x * lax.rsqrt(jnp.mean(x * x, axis=-1, keepdims=True) + EPS) * g


def _dot(a, b):
    return jnp.dot(a, b, preferred_element_type=F32)


def _dot_nt(a, b):
    return lax.dot_general(a, b, (((1,), (1,)), ((), ())), preferred_element_type=F32)


def _prep_kernel(lre_ref, lim_ref, ldt_ref, bre_ref, bim_ref, lq1_ref, lk1_ref, lq2_ref, lk2_ref,
                 are_ref, aim_ref, bbre_ref, bbim_ref, lam_ref):
    lre, lim = lre_ref[...], lim_ref[...]
    dt = jnp.exp(ldt_ref[...])
    mag = jnp.exp(lre * dt)
    ar, ai = mag * jnp.cos(lim * dt), mag * jnp.sin(lim * dt)
    den = lre * lre + lim * lim
    fr = ((ar - 1.0) * lre + ai * lim) / den
    fi = (ai * lre - (ar - 1.0) * lim) / den
    are_ref[...] = ar
    aim_ref[...] = ai
    bre, bim = bre_ref[...], bim_ref[...]
    bbre_ref[...] = fr[:, None, :] * bre - fi[:, None, :] * bim
    bbim_ref[...] = fr[:, None, :] * bim + fi[:, None, :] * bre
    s1 = jnp.sum(lq1_ref[...] * lk1_ref[...], axis=-1, keepdims=True)
    s2 = jnp.sum(lq2_ref[...] * lk2_ref[...], axis=-1, keepdims=True)
    lam_ref[...] = jnp.exp(s1) - jnp.exp(s2) + LAM_INIT


def _prep(lam_re, lam_im, log_dt, b_re, b_im, lq1, lk1, lq2, lk2):
    g, p, c = SSM_GROUPS, SSM_STATE, SSM_GROUP
    out_shape = (jax.ShapeDtypeStruct((g, p), F32), jax.ShapeDtypeStruct((g, p), F32),
                 jax.ShapeDtypeStruct((g, c, p), F32), jax.ShapeDtypeStruct((g, c, p), F32),
                 jax.ShapeDtypeStruct((1, 1), F32))
    return pl.pallas_call(_prep_kernel, out_shape=out_shape, name="ssm_prep")(
        lam_re, lam_im, log_dt.reshape(g, 1), jnp.swapaxes(b_re, 1, 2), jnp.swapaxes(b_im, 1, 2),
        lq1.reshape(1, HEAD_DIM), lk1.reshape(1, HEAD_DIM), lq2.reshape(1, HEAD_DIM), lk2.reshape(1, HEAD_DIM))


def _block_diag(blocks):
    g, r, c = blocks.shape
    eye = jnp.eye(g, dtype=blocks.dtype)
    return (blocks[:, :, None, :] * eye[:, None, :, None]).reshape(g * r, g * c)


def _in_proj_kernel(x_ref, g_ref, w_ref, cos_ref, sa_ref, sb_ref,
                    u_ref, q_ref, kf_ref, kb_ref, vf_ref, vb_ref, gs_ref, ga_ref):
    h = _rms(x_ref[...], g_ref[...]).astype(BF16)
    o0 = SSM_WIDTH
    o1, o2, o3, o4 = o0 + ATTN_WIDTH, o0 + 2 * ATTN_WIDTH, o0 + 3 * ATTN_WIDTH, o0 + 3 * ATTN_WIDTH + D_MODEL

    u_ref[...] = _dot(h, w_ref[:, 0:o0]).astype(BF16)
    cos, sa, sb = cos_ref[...], sa_ref[...], sb_ref[...]

    def rope(z):
        return z * cos + pltpu.roll(z, LANES - ROPE_DIM // 2, 1) * sa + pltpu.roll(z, ROPE_DIM // 2, 1) * sb

    tm = x_ref.shape[0]
    zq = _dot(h, w_ref[:, o0:o1])
    for hh in range(N_HEADS):
        sl = slice(hh * QK_DIM, (hh + 1) * QK_DIM)
        q_ref[:, sl] = (rope(zq[:, sl]) * Q_SCALE).astype(BF16)
    zk = _dot(h, w_ref[:, o1:o2])
    for hh in range(N_HEADS):
        sl = slice(hh * QK_DIM, (hh + 1) * QK_DIM)
        kr = rope(zk[:, sl])
        kf_ref[pl.ds(hh, tm, stride=N_HEADS), :] = kr
        kb_ref[:, sl] = kr.astype(BF16)
    zv = _dot(h, w_ref[:, o2:o3])
    for hh in range(N_HEADS):
        vf_ref[pl.ds(hh, tm, stride=N_HEADS), :] = zv[:, hh * QK_DIM:(hh + 1) * QK_DIM]
    vb_ref[...] = zv.astype(BF16)
    gs_ref[...] = jax.nn.sigmoid(_dot(h, w_ref[:, o3:o4])).astype(BF16)
    ga_ref[...] = jax.nn.sigmoid(_dot(h, w_ref[:, o4:IN_WIDTH])).astype(BF16)


def _rope_tables(pos):
    half = ROPE_DIM // 2
    inv = ROPE_THETA ** (-np.arange(half, dtype=np.float64) * 2.0 / ROPE_DIM)
    ang = np.asarray(pos, np.float64)[:, None] * inv[None, :]
    cos, sin = np.cos(ang), np.sin(ang)
    n = ang.shape[0]
    comp_cos = np.concatenate([cos, cos, np.ones((n, HEAD_DIM - ROPE_DIM))], axis=1)
    comp_sa = np.concatenate([-sin, np.zeros((n, HEAD_DIM - half))], axis=1)
    comp_sb = np.concatenate([np.zeros((n, half)), sin, np.zeros((n, HEAD_DIM - ROPE_DIM))], axis=1)
    two = lambda t: jnp.asarray(np.concatenate([t, t], axis=1), F32)
    return two(comp_cos), two(comp_sa), two(comp_sb)


def _in_proj(x2d, pos, nb, g_pre, w_in_bf, tm):
    m = x2d.shape[0]
    t_len = m // nb
    nt = t_len // tm
    cos, sa, sb = _rope_tables(pos)
    row = lambda w: pl.BlockSpec((tm, w), lambda i: (i, 0))
    heads = pl.BlockSpec((tm * N_HEADS, QK_DIM), lambda i: (i, 0))
    tab = pl.BlockSpec((tm, LANES), lambda i: (i % nt, 0))
    out_shape = (jax.ShapeDtypeStruct((t_len, nb * SSM_WIDTH), BF16),
                 jax.ShapeDtypeStruct((m, ATTN_WIDTH), BF16),
                 jax.ShapeDtypeStruct((m * N_HEADS, QK_DIM), F32), jax.ShapeDtypeStruct((m, ATTN_WIDTH), BF16),
                 jax.ShapeDtypeStruct((m * N_HEADS, QK_DIM), F32), jax.ShapeDtypeStruct((m, ATTN_WIDTH), BF16),
                 jax.ShapeDtypeStruct((m, D_MODEL), BF16), jax.ShapeDtypeStruct((m, D_MODEL), BF16))
    out_specs = (pl.BlockSpec((tm, SSM_WIDTH), lambda i: (i % nt, i // nt)),
                 row(ATTN_WIDTH), heads, row(ATTN_WIDTH), heads, row(ATTN_WIDTH),
                 row(D_MODEL), row(D_MODEL))
    return pl.pallas_call(
        _in_proj_kernel, grid=(m // tm,),
        in_specs=[row(D_MODEL), _resident((1, D_MODEL)), _resident((D_MODEL, IN_WIDTH)), tab, tab, tab],
        out_specs=out_specs, out_shape=out_shape,
        compiler_params=_params(("parallel",)), name="in_proj")(x2d, g_pre, w_in_bf, cos, sa, sb)


def _ssm_drive(u, bd_ref, store):
    n_tiles = 2 * SSM_LANES // 256
    for j in range(n_tiles):
        ks = LANES * ((j % (n_tiles // 2)) // 2)
        lanes = slice(256 * j, 256 * (j + 1))
        store(lanes, _dot(u[:, ks:ks + LANES], bd_ref[ks:ks + LANES, lanes]))


def _ssm_readout(load, u, cre_ref, cim_ref, dsk_ref):
    half = SSM_LANES // 2
    ys = []
    for j in range(2):
        hre = load(slice(half * j, half * (j + 1))).astype(BF16)
        him = load(slice(SSM_LANES + half * j, SSM_LANES + half * (j + 1))).astype(BF16)
        cols = slice(256 * j, 256 * (j + 1))
        ys.append(_dot(hre, cre_ref[half * j:half * (j + 1), cols]) + _dot(him, cim_ref[half * j:half * (j + 1), cols]))
    y = jnp.concatenate(ys, axis=1) + dsk_ref[...] * u.astype(F32)
    return jax.nn.gelu(y).astype(BF16)


def _glu(yg, wglu_ref):
    return _dot(yg, wglu_ref[:, 0:D_MODEL]) * jax.nn.sigmoid(_dot(yg, wglu_ref[:, D_MODEL:2 * D_MODEL]))


def _ssm_prompt_kernel(u_ref, are_ref, aim_ref, bd_ref, cre_ref, cim_ref, dsk_ref, wglu_ref, perm_ref, permt_ref,
                       ys_ref, hre_ref, him_ref, h_s, st_re, st_im, *, nb, lc):
    @pl.when(pl.program_id(0) == 0)
    def _():
        st_re[...] = jnp.zeros_like(st_re)
        st_im[...] = jnp.zeros_like(st_im)

    rows = lc * nb
    n_tr = rows // SUBLANES
    u_bt = jnp.concatenate([u_ref[:, b * SSM_WIDTH:(b + 1) * SSM_WIDTH] for b in range(nb)], axis=0)
    u = _dot(permt_ref[...], u_bt).astype(BF16)

    def store(lanes, val):
        h_s[:, :, lanes] = val.reshape(n_tr, SUBLANES, val.shape[-1])

    _ssm_drive(u, bd_ref, store)

    def step(s, carry):
        hr, hi = carry
        ar, ai = are_ref[...], aim_ref[...]
        for k in range(SUBLANES // nb):
            sub = slice(k * nb, (k + 1) * nb)
            nhr = ar * hr - ai * hi + h_s[s, sub, 0:SSM_LANES]
            nhi = ar * hi + ai * hr + h_s[s, sub, SSM_LANES:2 * SSM_LANES]
            h_s[s, sub, 0:SSM_LANES] = nhr
            h_s[s, sub, SSM_LANES:2 * SSM_LANES] = nhi
            hr, hi = nhr, nhi
        return hr, hi

    hr, hi = lax.fori_loop(0, n_tr, step, (st_re[...], st_im[...]))
    st_re[...] = hr
    st_im[...] = hi
    hre_ref[...] = hr
    him_ref[...] = hi

    load = lambda lanes: h_s[:, :, lanes].reshape(rows, lanes.stop - lanes.start)
    yg = _ssm_readout(load, u, cre_ref, cim_ref, dsk_ref)
    ygp = _dot(perm_ref[...], yg).astype(BF16)
    res = _glu(ygp, wglu_ref)
    for b in range(nb):
        ys_ref[b] = res[b * lc:(b + 1) * lc].astype(BF16)


def _ssm_prompt(u_il, nb, t_len, are, aim, bd, cre, cim, dsk, wglu, lc):
    rows = lc * nb
    idx = np.arange(rows)
    perm = jnp.asarray(idx[None, :] == ((idx % lc) * nb + idx // lc)[:, None], BF16)
    kern = functools.partial(_ssm_prompt_kernel, nb=nb, lc=lc)
    out_shape = (jax.ShapeDtypeStruct((nb, t_len, D_MODEL), BF16),
                 jax.ShapeDtypeStruct((nb, SSM_LANES), F32), jax.ShapeDtypeStruct((nb, SSM_LANES), F32))
    return pl.pallas_call(
        kern, grid=(t_len // lc,),
        in_specs=[pl.BlockSpec((lc, nb * SSM_WIDTH), lambda c: (c, 0)),
                  _resident((nb, SSM_LANES)), _resident((nb, SSM_LANES)),
                  _resident((SSM_WIDTH, 2 * SSM_LANES)), _resident((SSM_LANES, SSM_WIDTH)),
                  _resident((SSM_LANES, SSM_WIDTH)), _resident((1, SSM_WIDTH)),
                  _resident((SSM_WIDTH, 2 * D_MODEL)), _resident((rows, rows)), _resident((rows, rows))],
        out_specs=(pl.BlockSpec((nb, lc, D_MODEL), lambda c: (0, c, 0)),
                   pl.BlockSpec((nb, SSM_LANES), lambda c: (0, 0)), pl.BlockSpec((nb, SSM_LANES), lambda c: (0, 0))),
        out_shape=out_shape,
        scratch_shapes=[pltpu.VMEM((rows // SUBLANES, SUBLANES, 2 * SSM_LANES), F32),
                        pltpu.VMEM((nb, SSM_LANES), F32), pltpu.VMEM((nb, SSM_LANES), F32)],
        compiler_params=_params(("arbitrary",)), name="ssm_prompt")(
            u_il, jnp.tile(are, (nb, 1)), jnp.tile(aim, (nb, 1)), bd, cre, cim, dsk, wglu, perm, perm.T)


def _ssm_step_kernel(u_ref, h0re_ref, h0im_ref, are_ref, aim_ref, bd_ref, cre_ref, cim_ref, dsk_ref, wglu_ref,
                     ys_ref, hre_ref, him_ref, h_s):
    u = u_ref[...]

    def store(lanes, val):
        h_s[:, lanes] = val

    _ssm_drive(u, bd_ref, store)
    ar, ai = are_ref[...], aim_ref[...]
    hr, hi = h0re_ref[...], h0im_ref[...]
    nhr = ar * hr - ai * hi + h_s[:, 0:SSM_LANES]
    nhi = ar * hi + ai * hr + h_s[:, SSM_LANES:2 * SSM_LANES]
    hre_ref[...] = nhr
    him_ref[...] = nhi
    h_s[:, 0:SSM_LANES] = nhr
    h_s[:, SSM_LANES:2 * SSM_LANES] = nhi
    yg = _ssm_readout(lambda lanes: h_s[:, lanes], u, cre_ref, cim_ref, dsk_ref)
    ys_ref[...] = _glu(yg, wglu_ref).astype(BF16)


def _ssm_step(u, h0re, h0im, are, aim, bd, cre, cim, dsk, wglu):
    n = u.shape[0]
    out_shape = (jax.ShapeDtypeStruct((n, D_MODEL), BF16),
                 jax.ShapeDtypeStruct((n, SSM_LANES), F32), jax.ShapeDtypeStruct((n, SSM_LANES), F32))
    return pl.pallas_call(
        _ssm_step_kernel, out_shape=out_shape,
        scratch_shapes=[pltpu.VMEM((n, 2 * SSM_LANES), F32)],
        compiler_params=pltpu.CompilerParams(vmem_limit_bytes=VMEM_LIMIT), name="ssm_step")(
            u, h0re, h0im, are, aim, bd, cre, cim, dsk, wglu)


def _subln(o1, l1, o2, l2, lam, gain):
    o = o1 / l1 - lam * (o2 / l2)
    return _rms(o, gain) * (1.0 - LAM_INIT)


def _prompt_tile(qi, lam_ref, q_ref, k_ref, v_ref, gain_ref, o_ref, sa_s, sb_s, m_s, acc_s, tq, tk):
    lane = lax.broadcasted_iota(jnp.int32, (tk, QK_DIM), 1)
    parts = []
    for half in range(2):
        q = q_ref[half * tk:(half + 1) * tk, :]
        zero = jnp.zeros_like(q)
        parts += [jnp.where(lane < HEAD_DIM, q, zero), jnp.where(lane >= HEAD_DIM, q, zero)]
    qz = jnp.concatenate(parts, axis=0)
    m_s[...] = jnp.full_like(m_s, -jnp.inf)
    acc_s[...] = jnp.zeros_like(acc_s)
    ones = jnp.ones((tk, LANES), BF16)
    wide = lambda a: jnp.concatenate([a] * (tk // LANES), axis=1)

    def scores(c, s_ref, rows=slice(0, 2 * tq)):
        r0 = pl.multiple_of(c * tk, tk)
        s_ref[rows, :] = _dot_nt(qz[rows], k_ref[pl.ds(r0, tk), :])

    def softmax_pv(c, s_ref, rows=slice(0, 2 * tq), diag_rows=0):
        s = s_ref[rows, :]
        if diag_rows:
            row = lax.broadcasted_iota(jnp.int32, (diag_rows, tk), 0) % tk
            col = lax.broadcasted_iota(jnp.int32, (diag_rows, tk), 1)
            masked = jnp.where(col <= row, s[0:diag_rows], -jnp.inf)
            s = masked if diag_rows == s.shape[0] else jnp.concatenate([masked, s[diag_rows:]], axis=0)
        m_prev = m_s[rows, :]
        m_new = jnp.maximum(m_prev, jnp.max(s, axis=1, keepdims=True))
        alpha = jnp.exp2(m_prev - m_new)
        p = jnp.exp2(s - wide(m_new)).astype(BF16)
        r0 = pl.multiple_of(c * tk, tk)
        v1 = jnp.concatenate([v_ref[pl.ds(r0, tk), :], ones], axis=1)
        acc_s[rows, :] = jnp.concatenate([alpha, alpha], axis=1) * acc_s[rows, :] + _dot(p, v1)
        m_s[rows, :] = m_new

    scores(0, sa_s)

    def body(j, carry):
        c0 = 2 * j
        scores(c0 + 1, sb_s)
        softmax_pv(c0, sa_s)
        scores(c0 + 2, sa_s)
        softmax_pv(c0 + 1, sb_s)
        return carry

    lax.fori_loop(0, qi, body, 0)
    c0 = 2 * qi
    late = slice(tq, 2 * tq)
    scores(c0 + 1, sb_s, late)
    softmax_pv(c0, sa_s, diag_rows=tq)
    softmax_pv(c0 + 1, sb_s, late, diag_rows=tq)
    acc = acc_s[...]
    lam, gain = lam_ref[0, 0], gain_ref[...]
    for half in range(2):
        a1 = acc[2 * half * tk:(2 * half + 1) * tk]
        a2 = acc[(2 * half + 1) * tk:(2 * half + 2) * tk]
        o_ref[half * tk:(half + 1) * tk, :] = _subln(a1[:, 0:QK_DIM], a1[:, QK_DIM:2 * QK_DIM], a2[:, 0:QK_DIM],
                                                     a2[:, QK_DIM:2 * QK_DIM], lam, gain).astype(BF16)


def _decode_sample(lam_ref, gain_ref, q_ref, k_refs, v_refs, kn_ref, vn_ref, o_ref):
    n_maps = 2 * N_HEADS
    rows_pp = PAGE_SIZE * N_HEADS
    q8 = q_ref[0].astype(F32)
    lane = lax.broadcasted_iota(jnp.int32, (N_HEADS, QK_DIM), 1)
    qz32 = jnp.concatenate([jnp.where(lane < HEAD_DIM, q8, 0.0), jnp.where(lane >= HEAD_DIM, q8, 0.0)], axis=0)
    qz = qz32.astype(BF16)
    per = len(k_refs) // DECODE_GROUPS
    col = lax.broadcasted_iota(jnp.int32, (n_maps, per * rows_pp), 1)
    row = lax.broadcasted_iota(jnp.int32, (n_maps, per * rows_pp), 0)
    own_head = col % N_HEADS == row % N_HEADS

    groups = []
    for g in range(DECODE_GROUPS):
        ks, vs = k_refs[g * per:(g + 1) * per], v_refs[g * per:(g + 1) * per]
        s = jnp.concatenate([_dot_nt(qz, r[...].astype(BF16)) for r in ks], axis=1)
        s = jnp.where(own_head, s, -jnp.inf)
        m = jnp.max(s, axis=1, keepdims=True)
        p = jnp.exp2(s - m)
        pb = p.astype(BF16)
        pv = _dot(pb[:, 0:rows_pp], vs[0][...].astype(BF16))
        for i in range(1, per):
            pv = pv + _dot(pb[:, i * rows_pp:(i + 1) * rows_pp], vs[i][...].astype(BF16))
        groups.append((m, jnp.sum(p, axis=1, keepdims=True), pv))

    two = lambda a: jnp.concatenate([a, a], axis=0)
    s_n = jnp.sum(qz32 * two(kn_ref[0].astype(F32)), axis=1, keepdims=True)
    m = s_n
    for mg, _, _ in groups:
        m = jnp.maximum(m, mg)
    l = jnp.exp2(s_n - m)
    acc = l * two(vn_ref[0].astype(F32))
    for mg, lg, pvg in groups:
        w = jnp.exp2(mg - m)
        l = l + w * lg
        acc = acc + w * pvg
    o_ref[0] = _subln(acc[0:N_HEADS], l[0:N_HEADS], acc[N_HEADS:n_maps], l[N_HEADS:n_maps], lam_ref[0, 0],
                      gain_ref[...]).astype(BF16)


def _attn_kernel(pt_ref, lam_ref, q_ref, k_ref, v_ref, gain_ref, qs_ref, *refs, tq, tk, n_pages):
    del pt_ref
    k_refs, v_refs = refs[0:n_pages], refs[n_pages:2 * n_pages]
    kn_ref, vn_ref, o_ref, os_ref, sa_s, sb_s, m_s, acc_s = refs[2 * n_pages:]
    _decode_sample(lam_ref, gain_ref, qs_ref, k_refs, v_refs, kn_ref, vn_ref, os_ref)
    _prompt_tile(pl.program_id(2), lam_ref, q_ref, k_ref, v_ref, gain_ref, o_ref, sa_s, sb_s, m_s, acc_s, tq, tk)


def _attention(page_table, lam, q, k, v, gain, q_s, cache_k, cache_v, k_new, v_new, nb, t_len, tq):
    tk = tq // 2
    n, n_pages = page_table.shape
    nq = t_len // tq
    assert nb * N_HEADS * nq == n, "one decode sample per prompt grid step"
    kern = functools.partial(_attn_kernel, tq=tq, tk=tk, n_pages=n_pages)
    sample = lambda b, h, i: (b * N_HEADS + h) * nq + i
    qspec = pl.BlockSpec((None, tq, QK_DIM), lambda b, h, i, pt: (b, i, h))
    kvspec = pl.BlockSpec((None, t_len, QK_DIM), lambda b, h, i, pt: (b, 0, h))
    row = pl.BlockSpec((1, N_HEADS, QK_DIM), lambda b, h, i, pt: (sample(b, h, i), 0, 0))

    def page(j):
        return pl.BlockSpec((None, PAGE_SIZE * N_HEADS, QK_DIM), lambda b, h, i, pt: (pt[sample(b, h, i), j], 0, 0))

    pages = [page(j) for j in range(n_pages)]
    grid_spec = pltpu.PrefetchScalarGridSpec(
        num_scalar_prefetch=1, grid=(nb, N_HEADS, nq),
        in_specs=[pl.BlockSpec(memory_space=pltpu.SMEM), qspec, kvspec, kvspec,
                  pl.BlockSpec((1, QK_DIM), lambda b, h, i, pt: (0, 0)), row] + pages + pages + [row, row],
        out_specs=(qspec, row),
        scratch_shapes=[pltpu.VMEM((2 * tq, tk), F32), pltpu.VMEM((2 * tq, tk), F32),
                        pltpu.VMEM((2 * tq, LANES), F32), pltpu.VMEM((2 * tq, 2 * QK_DIM), F32)])
    r3 = lambda a: a.reshape(n, N_HEADS, QK_DIM)
    out_shape = (jax.ShapeDtypeStruct((nb, t_len, ATTN_WIDTH), BF16), jax.ShapeDtypeStruct((n, N_HEADS, QK_DIM), BF16))
    return pl.pallas_call(
        kern, grid_spec=grid_spec, out_shape=out_shape,
        compiler_params=_params(("arbitrary", "arbitrary", "arbitrary"), ATTN_VMEM_LIMIT), name="attention")(
            page_table, lam, q, k, v, gain, r3(q_s), *([cache_k] * n_pages), *([cache_v] * n_pages),
            r3(k_new), r3(v_new))


FF_CHUNKS = ((0, 1024), (1024, 2048), (2048, D_FF))


def _out_ffn_kernel(x_ref, ys_ref, o_ref, gs_ref, ga_ref, wo_ref, npm_ref, npf_ref, wg_ref, wu_ref, wd_ref, nof_ref,
                    y_ref):
    mixed = gs_ref[...].astype(F32) * ys_ref[...].astype(F32) + ga_ref[...].astype(F32) * o_ref[...].astype(F32)
    x1 = x_ref[...] + _rms(_dot(mixed.astype(BF16), wo_ref[...]), npm_ref[...])
    hf = _rms(x1, npf_ref[...]).astype(BF16)
    f = None
    for c0, c1 in FF_CHUNKS:
        act = (jax.nn.silu(_dot(hf, wg_ref[:, c0:c1])) * _dot(hf, wu_ref[:, c0:c1])).astype(BF16)
        part = _dot(act, wd_ref[c0:c1, :])
        f = part if f is None else f + part
    y_ref[...] = x1 + _rms(f, nof_ref[...])


def _out_ffn(x2d, ys, o, gs, ga, wo, npm, npf, wg, wu, wd, nof, tm):
    m = x2d.shape[0]
    row = pl.BlockSpec((tm, D_MODEL), lambda i: (i, 0))
    vec = _resident((1, D_MODEL))
    return pl.pallas_call(
        _out_ffn_kernel, grid=(m // tm,),
        in_specs=[row, row, row, row, row, _resident((D_MODEL, D_MODEL)), vec, vec,
                  _resident((D_MODEL, D_FF)), _resident((D_MODEL, D_FF)), _resident((D_FF, D_MODEL)), vec],
        out_specs=row, out_shape=jax.ShapeDtypeStruct((m, D_MODEL), F32),
        compiler_params=_params(("parallel",)), name="out_ffn")(x2d, ys, o, gs, ga, wo, npm, npf, wg, wu, wd, nof)


def kernel(x_prompt, x_sample, cache_k, cache_v, state_ssm_re, state_ssm_im, page_table, norm_pre_mix, w_in,
           ssm_lambda_re, ssm_lambda_im, ssm_log_dt, ssm_b_re, ssm_b_im, ssm_c_re, ssm_c_im, ssm_d, w_glu_a,
           w_glu_b, lambda_q1, lambda_k1, lambda_q2, lambda_k2, subln_gain, w_o, norm_post_mix, norm_pre_ffn,
           w_gate, w_up, w_down, norm_post_ffn):
    assert w_in.shape[0] == 1, "single-layer stack"
    bp, tp, _ = x_prompt.shape
    bs, ts, _ = x_sample.shape
    assert ts == 1

    are, aim, bbre, bbim, lam = _prep(ssm_lambda_re[0], ssm_lambda_im[0], ssm_log_dt[0], ssm_b_re[0], ssm_b_im[0],
                                      lambda_q1[0], lambda_k1[0], lambda_q2[0], lambda_k2[0])
    are, aim = are.reshape(1, SSM_LANES), aim.reshape(1, SSM_LANES)
    bd = jnp.concatenate([_block_diag(bbre), _block_diag(bbim)], axis=1).astype(BF16)
    cre = _block_diag(jnp.swapaxes(ssm_c_re[0], 1, 2)).astype(BF16)
    cim = _block_diag(jnp.swapaxes(-ssm_c_im[0], 1, 2)).astype(BF16)
    dsk = ssm_d[0].reshape(1, SSM_WIDTH)
    wglu = jnp.concatenate([w_glu_a[0], w_glu_b[0]], axis=1).astype(BF16)
    vec = lambda a: a[0].reshape(1, -1)
    w_in_bf, wo_bf = w_in[0].astype(BF16), w_o[0].astype(BF16)
    wg_bf, wu_bf, wd_bf = w_gate[0].astype(BF16), w_up[0].astype(BF16), w_down[0].astype(BF16)
    gain = vec(subln_gain)
    ssm_w = (are, aim, bd, cre, cim, dsk, wglu)
    ffn_w = (wo_bf, vec(norm_post_mix), vec(norm_pre_ffn), wg_bf, wu_bf, wd_bf, vec(norm_post_ffn))

    xp = x_prompt.reshape(bp * tp, D_MODEL)
    u_il, q, kf, kb, vf, vb, gs, ga = _in_proj(xp, np.arange(tp), bp, vec(norm_pre_mix), w_in_bf, tm=512)
    ys, hpr, hpi = _ssm_prompt(u_il, bp, tp, *ssm_w, lc=128)

    xs = x_sample.reshape(bs, D_MODEL)
    pos_s = np.full((bs,), PAST_LEN)
    u_s, q_s, kf_s, kb_s, vf_s, vb_s, gs_s, ga_s = _in_proj(xs, pos_s, 1, vec(norm_pre_mix), w_in_bf, tm=bs)
    ys_s, hsr, hsi = _ssm_step(u_s, state_ssm_re[0].reshape(bs, SSM_LANES), state_ssm_im[0].reshape(bs, SSM_LANES),
                               *ssm_w)

    n_phys = cache_k.shape[1]
    pages = lambda c: c[0].reshape(n_phys, PAGE_SIZE * N_HEADS, QK_DIM)
    r3 = lambda a: a.reshape(bp, tp, ATTN_WIDTH)
    o, o_s = _attention(page_table, lam, r3(q), r3(kb), r3(vb), gain, q_s, pages(cache_k), pages(cache_v),
                        kb_s, vb_s, bp, tp, tq=1024)

    yp = _out_ffn(xp, ys.reshape(bp * tp, D_MODEL), o.reshape(bp * tp, ATTN_WIDTH), gs, ga, *ffn_w, tm=512)
    ysamp = _out_ffn(xs, ys_s, o_s.reshape(bs, ATTN_WIDTH), gs_s, ga_s, *ffn_w, tm=bs)

    st = lambda a, n: a.reshape(1, n, SSM_GROUPS, SSM_STATE)
    kv = lambda a, n, t: a.reshape(1, n, t, N_HEADS, QK_DIM)
    return (yp.reshape(bp, tp, D_MODEL), ysamp.reshape(bs, ts, D_MODEL),
            kv(kf, bp, tp), kv(vf, bp, tp), st(hpr, bp), st(hpi, bp),
            kv(kf_s, bs, ts), kv(vf_s, bs, ts), st(hsr, bs), st(hsi, bs))
```

```python
import functools
import math

import jax
import jax.numpy as jnp
import numpy as np
from jax import lax
from jax.experimental import pallas as pl
from jax.experimental.pallas import tpu as pltpu

F32 = jnp.float32
BF16 = jnp.bfloat16

D_MODEL = 1024
PAST_LEN = 2048
PAGE_SIZE = 128
SSM_WIDTH = 512
SSM_GROUP = 16
SSM_GROUPS = 32
SSM_STATE = 64
SSM_LANES = SSM_GROUPS * SSM_STATE
HEAD_DIM = 64
N_HEADS = 8
QK_DIM = 128
ATTN_WIDTH = 1024
ROPE_DIM = 16
ROPE_THETA = 500000.0
D_FF = 2816
IN_WIDTH = SSM_WIDTH + 3 * ATTN_WIDTH + 2 * D_MODEL
EPS = 1e-6
LAM_INIT = 0.8 - 0.6 * math.exp(-0.3 * 0)
Q_SCALE = HEAD_DIM ** -0.5 * math.log2(math.e)

LANES = 128
SUBLANES = 8
VMEM_LIMIT = 56 * 1024 * 1024
ATTN_VMEM_LIMIT = 62 * 1024 * 1024


def _params(sem, vmem=VMEM_LIMIT):
    return pltpu.CompilerParams(dimension_semantics=sem, vmem_limit_bytes=vmem)


def _resident(shape):
    return pl.BlockSpec(shape, lambda *_: (0,) * len(shape), pipeline_mode=pl.Buffered(1))


def _rms(x, g):
    return x * lax.rsqrt(jnp.mean(x * x, axis=-1, keepdims=True) + EPS) * g


def _dot(a, b):
    return jnp.dot(a, b, preferred_element_type=F32)


def _dot_nt(a, b):
    return lax.dot_general(a, b, (((1,), (1,)), ((), ())), preferred_element_type=F32)


def _prep_kernel(lre_ref, lim_ref, ldt_ref, bre_ref, bim_ref, lq1_ref, lk1_ref, lq2_ref, lk2_ref,
                 are_ref, aim_ref, bbre_ref, bbim_ref, lam_ref):
    lre, lim = lre_ref[...], lim_ref[...]
    dt = jnp.exp(ldt_ref[...])
    mag = jnp.exp(lre * dt)
    ar, ai = mag * jnp.cos(lim * dt), mag * jnp.sin(lim * dt)
    den = lre * lre + lim * lim
    fr = ((ar - 1.0) * lre + ai * lim) / den
    fi = (ai * lre - (ar - 1.0) * lim) / den
    are_ref[...] = ar
    aim_ref[...] = ai
    bre, bim = bre_ref[...], bim_ref[...]
    bbre_ref[...] = fr[:, None, :] * bre - fi[:, None, :] * bim
    bbim_ref[...] = fr[:, None, :] * bim + fi[:, None, :] * bre
    s1 = jnp.sum(lq1_ref[...] * lk1_ref[...], axis=-1, keepdims=True)
    s2 = jnp.sum(lq2_ref[...] * lk2_ref[...], axis=-1, keepdims=True)
    lam_ref[...] = jnp.exp(s1) - jnp.exp(s2) + LAM_INIT


def _prep(lam_re, lam_im, log_dt, b_re, b_im, lq1, lk1, lq2, lk2):
    g, p, c = SSM_GROUPS, SSM_STATE, SSM_GROUP
    out_shape = (jax.ShapeDtypeStruct((g, p), F32), jax.ShapeDtypeStruct((g, p), F32),
                 jax.ShapeDtypeStruct((g, c, p), F32), jax.ShapeDtypeStruct((g, c, p), F32),
                 jax.ShapeDtypeStruct((1, 1), F32))
    return pl.pallas_call(_prep_kernel, out_shape=out_shape, name="ssm_prep")(
        lam_re, lam_im, log_dt.reshape(g, 1), jnp.swapaxes(b_re, 1, 2), jnp.swapaxes(b_im, 1, 2),
        lq1.reshape(1, HEAD_DIM), lk1.reshape(1, HEAD_DIM), lq2.reshape(1, HEAD_DIM), lk2.reshape(1, HEAD_DIM))


def _block_diag(blocks):
    g, r, c = blocks.shape
    keep = np.arange(g * r)[:, None] // r == np.arange(g * c)[None, :] // c
    return jnp.where(keep, jnp.tile(blocks.reshape(g * r, c), (1, g)), 0.0)


def _in_proj_kernel(x_ref, g_ref, w_ref, cos_ref, sa_ref, sb_ref,
                    u_ref, q_ref, kf_ref, kb_ref, vf_ref, vb_ref, gs_ref, ga_ref):
    h = _rms(x_ref[...], g_ref[...]).astype(BF16)
    o0 = SSM_WIDTH
    o1, o2, o3, o4 = o0 + ATTN_WIDTH, o0 + 2 * ATTN_WIDTH, o0 + 3 * ATTN_WIDTH, o0 + 3 * ATTN_WIDTH + D_MODEL

    u_ref[...] = _dot(h, w_ref[:, 0:o0]).astype(BF16)
    cos, sa, sb = cos_ref[...], sa_ref[...], sb_ref[...]

    def rope(z):
        return z * cos + pltpu.roll(z, LANES - ROPE_DIM // 2, 1) * sa + pltpu.roll(z, ROPE_DIM // 2, 1) * sb

    tm = x_ref.shape[0]
    zq = _dot(h, w_ref[:, o0:o1])
    for hh in range(N_HEADS):
        sl = slice(hh * QK_DIM, (hh + 1) * QK_DIM)
        q_ref[:, sl] = (rope(zq[:, sl]) * Q_SCALE).astype(BF16)
    zk = _dot(h, w_ref[:, o1:o2])
    for hh in range(N_HEADS):
        sl = slice(hh * QK_DIM, (hh + 1) * QK_DIM)
        kr = rope(zk[:, sl])
        kf_ref[pl.ds(hh, tm, stride=N_HEADS), :] = kr
        kb_ref[:, sl] = kr.astype(BF16)
    zv = _dot(h, w_ref[:, o2:o3])
    for hh in range(N_HEADS):
        vf_ref[pl.ds(hh, tm, stride=N_HEADS), :] = zv[:, hh * QK_DIM:(hh + 1) * QK_DIM]
    vb_ref[...] = zv.astype(BF16)
    gs_ref[...] = jax.nn.sigmoid(_dot(h, w_ref[:, o3:o4])).astype(BF16)
    ga_ref[...] = jax.nn.sigmoid(_dot(h, w_ref[:, o4:IN_WIDTH])).astype(BF16)


def _rope_tables(pos):
    half = ROPE_DIM // 2
    inv = ROPE_THETA ** (-np.arange(half, dtype=np.float64) * 2.0 / ROPE_DIM)
    ang = np.asarray(pos, np.float64)[:, None] * inv[None, :]
    cos, sin = np.cos(ang), np.sin(ang)
    n = ang.shape[0]
    comp_cos = np.concatenate([cos, cos, np.ones((n, HEAD_DIM - ROPE_DIM))], axis=1)
    comp_sa = np.concatenate([-sin, np.zeros((n, HEAD_DIM - half))], axis=1)
    comp_sb = np.concatenate([np.zeros((n, half)), sin, np.zeros((n, HEAD_DIM - ROPE_DIM))], axis=1)
    two = lambda t: jnp.asarray(np.concatenate([t, t], axis=1), F32)
    return two(comp_cos), two(comp_sa), two(comp_sb)


def _in_proj(x2d, pos, nb, g_pre, w_in_bf, tm):
    m = x2d.shape[0]
    t_len = m // nb
    nt = t_len // tm
    cos, sa, sb = _rope_tables(pos)
    row = lambda w: pl.BlockSpec((tm, w), lambda i: (i, 0))
    heads = pl.BlockSpec((tm * N_HEADS, QK_DIM), lambda i: (i, 0))
    tab = pl.BlockSpec((tm, LANES), lambda i: (i % nt, 0))
    out_shape = (jax.ShapeDtypeStruct((t_len, nb * SSM_WIDTH), BF16),
                 jax.ShapeDtypeStruct((m, ATTN_WIDTH), BF16),
                 jax.ShapeDtypeStruct((m * N_HEADS, QK_DIM), F32), jax.ShapeDtypeStruct((m, ATTN_WIDTH), BF16),
                 jax.ShapeDtypeStruct((m * N_HEADS, QK_DIM), F32), jax.ShapeDtypeStruct((m, ATTN_WIDTH), BF16),
                 jax.ShapeDtypeStruct((m, D_MODEL), BF16), jax.ShapeDtypeStruct((m, D_MODEL), BF16))
    out_specs = (pl.BlockSpec((tm, SSM_WIDTH), lambda i: (i % nt, i // nt)),
                 row(ATTN_WIDTH), heads, row(ATTN_WIDTH), heads, row(ATTN_WIDTH),
                 row(D_MODEL), row(D_MODEL))
    return pl.pallas_call(
        _in_proj_kernel, grid=(m // tm,),
        in_specs=[row(D_MODEL), _resident((1, D_MODEL)), _resident((D_MODEL, IN_WIDTH)), tab, tab, tab],
        out_specs=out_specs, out_shape=out_shape,
        compiler_params=_params(("parallel",)), name="in_proj")(x2d, g_pre, w_in_bf, cos, sa, sb)


def _ssm_drive(u, bd_ref, store):
    n_tiles = 2 * SSM_LANES // 256
    for j in range(n_tiles):
        ks = LANES * ((j % (n_tiles // 2)) // 2)
        lanes = slice(256 * j, 256 * (j + 1))
        store(lanes, _dot(u[:, ks:ks + LANES], bd_ref[ks:ks + LANES, lanes]))


def _ssm_readout(load, u, cre_ref, cim_ref, dsk_ref):
    half = SSM_LANES // 2
    ys = []
    for j in range(2):
        hre = load(slice(half * j, half * (j + 1))).astype(BF16)
        him = load(slice(SSM_LANES + half * j, SSM_LANES + half * (j + 1))).astype(BF16)
        cols = slice(256 * j, 256 * (j + 1))
        ys.append(_dot(hre, cre_ref[half * j:half * (j + 1), cols]) + _dot(him, cim_ref[half * j:half * (j + 1), cols]))
    y = jnp.concatenate(ys, axis=1) + dsk_ref[...] * u.astype(F32)
    return jax.nn.gelu(y).astype(BF16)


def _glu(yg, wglu_ref):
    return _dot(yg, wglu_ref[:, 0:D_MODEL]) * jax.nn.sigmoid(_dot(yg, wglu_ref[:, D_MODEL:2 * D_MODEL]))


def _ssm_prompt_kernel(u_ref, are_ref, aim_ref, bd_ref, cre_ref, cim_ref, dsk_ref, wglu_ref, perm_ref, permt_ref,
                       ys_ref, hre_ref, him_ref, h_s, st_re, st_im, *, nb, lc):
    @pl.when(pl.program_id(0) == 0)
    def _():
        st_re[...] = jnp.zeros_like(st_re)
        st_im[...] = jnp.zeros_like(st_im)

    rows = lc * nb
    n_tr = rows // SUBLANES
    u_bt = jnp.concatenate([u_ref[:, b * SSM_WIDTH:(b + 1) * SSM_WIDTH] for b in range(nb)], axis=0)
    u = _dot(permt_ref[...], u_bt).astype(BF16)

    def store(lanes, val):
        h_s[:, :, lanes] = val.reshape(n_tr, SUBLANES, val.shape[-1])

    _ssm_drive(u, bd_ref, store)

    def step(s, carry):
        hr, hi = carry
        ar, ai = are_ref[...], aim_ref[...]
        for k in range(SUBLANES // nb):
            sub = slice(k * nb, (k + 1) * nb)
            nhr = ar * hr - ai * hi + h_s[s, sub, 0:SSM_LANES]
            nhi = ar * hi + ai * hr + h_s[s, sub, SSM_LANES:2 * SSM_LANES]
            h_s[s, sub, 0:SSM_LANES] = nhr
            h_s[s, sub, SSM_LANES:2 * SSM_LANES] = nhi
            hr, hi = nhr, nhi
        return hr, hi

    hr, hi = lax.fori_loop(0, n_tr, step, (st_re[...], st_im[...]))
    st_re[...] = hr
    st_im[...] = hi
    hre_ref[...] = hr
    him_ref[...] = hi

    load = lambda lanes: h_s[:, :, lanes].reshape(rows, lanes.stop - lanes.start)
    yg = _ssm_readout(load, u, cre_ref, cim_ref, dsk_ref)
    ygp = _dot(perm_ref[...], yg).astype(BF16)
    res = _glu(ygp, wglu_ref)
    for b in range(nb):
        ys_ref[b] = res[b * lc:(b + 1) * lc].astype(BF16)


def _ssm_prompt(u_il, nb, t_len, are, aim, bd, cre, cim, dsk, wglu, lc):
    rows = lc * nb
    idx = np.arange(rows)
    perm = jnp.asarray(idx[None, :] == ((idx % lc) * nb + idx // lc)[:, None], BF16)
    kern = functools.partial(_ssm_prompt_kernel, nb=nb, lc=lc)
    out_shape = (jax.ShapeDtypeStruct((nb, t_len, D_MODEL), BF16),
                 jax.ShapeDtypeStruct((nb, SSM_LANES), F32), jax.ShapeDtypeStruct((nb, SSM_LANES), F32))
    return pl.pallas_call(
        kern, grid=(t_len // lc,),
        in_specs=[pl.BlockSpec((lc, nb * SSM_WIDTH), lambda c: (c, 0)),
                  _resident((nb, SSM_LANES)), _resident((nb, SSM_LANES)),
                  _resident((SSM_WIDTH, 2 * SSM_LANES)), _resident((SSM_LANES, SSM_WIDTH)),
                  _resident((SSM_LANES, SSM_WIDTH)), _resident((1, SSM_WIDTH)),
                  _resident((SSM_WIDTH, 2 * D_MODEL)), _resident((rows, rows)), _resident((rows, rows))],
        out_specs=(pl.BlockSpec((nb, lc, D_MODEL), lambda c: (0, c, 0)),
                   pl.BlockSpec((nb, SSM_LANES), lambda c: (0, 0)), pl.BlockSpec((nb, SSM_LANES), lambda c: (0, 0))),
        out_shape=out_shape,
        scratch_shapes=[pltpu.VMEM((rows // SUBLANES, SUBLANES, 2 * SSM_LANES), F32),
                        pltpu.VMEM((nb, SSM_LANES), F32), pltpu.VMEM((nb, SSM_LANES), F32)],
        compiler_params=_params(("arbitrary",)), name="ssm_prompt")(
            u_il, jnp.tile(are, (nb, 1)), jnp.tile(aim, (nb, 1)), bd, cre, cim, dsk, wglu, perm, perm.T)


def _ssm_step_kernel(u_ref, h0re_ref, h0im_ref, are_ref, aim_ref, bd_ref, cre_ref, cim_ref, dsk_ref, wglu_ref,
                     ys_ref, hre_ref, him_ref, h_s):
    u = u_ref[...]

    def store(lanes, val):
        h_s[:, lanes] = val

    _ssm_drive(u, bd_ref, store)
    ar, ai = are_ref[...], aim_ref[...]
    hr, hi = h0re_ref[...], h0im_ref[...]
    nhr = ar * hr - ai * hi + h_s[:, 0:SSM_LANES]
    nhi = ar * hi + ai * hr + h_s[:, SSM_LANES:2 * SSM_LANES]
    hre_ref[...] = nhr
    him_ref[...] = nhi
    h_s[:, 0:SSM_LANES] = nhr
    h_s[:, SSM_LANES:2 * SSM_LANES] = nhi
    yg = _ssm_readout(lambda lanes: h_s[:, lanes], u, cre_ref, cim_ref, dsk_ref)
    ys_ref[...] = _glu(yg, wglu_ref).astype(BF16)


def _ssm_step(u, h0re, h0im, are, aim, bd, cre, cim, dsk, wglu):
    n = u.shape[0]
    out_shape = (jax.ShapeDtypeStruct((n, D_MODEL), BF16),
                 jax.ShapeDtypeStruct((n, SSM_LANES), F32), jax.ShapeDtypeStruct((n, SSM_LANES), F32))
    return pl.pallas_call(
        _ssm_step_kernel, out_shape=out_shape,
        scratch_shapes=[pltpu.VMEM((n, 2 * SSM_LANES), F32)],
        compiler_params=pltpu.CompilerParams(vmem_limit_bytes=VMEM_LIMIT), name="ssm_step")(
            u, h0re, h0im, are, aim, bd, cre, cim, dsk, wglu)


def _subln(o1, l1, o2, l2, lam, gain):
    o = o1 / l1 - lam * (o2 / l2)
    return _rms(o, gain) * (1.0 - LAM_INIT)


def _prompt_tile(qi, lam_ref, q_ref, k_ref, v_ref, gain_ref, o_ref, sa_s, sb_s, m_s, acc_s, tq, tk,
                 before_loop, after_loop):
    lane = lax.broadcasted_iota(jnp.int32, (tk, QK_DIM), 1)
    parts = []
    for half in range(2):
        q = q_ref[half * tk:(half + 1) * tk, :]
        zero = jnp.zeros_like(q)
        parts += [jnp.where(lane < HEAD_DIM, q, zero), jnp.where(lane >= HEAD_DIM, q, zero)]
    qz = jnp.concatenate(parts, axis=0)
    m_s[...] = jnp.full_like(m_s, -jnp.inf)
    acc_s[...] = jnp.zeros_like(acc_s)
    ones = jnp.ones((tk, LANES), BF16)
    wide = lambda a: jnp.concatenate([a] * (tk // LANES), axis=1)

    def scores(c, s_ref, rows=slice(0, 2 * tq)):
        r0 = pl.multiple_of(c * tk, tk)
        s_ref[rows, :] = _dot_nt(qz[rows], k_ref[pl.ds(r0, tk), :])

    def softmax_pv(c, s_ref, rows=slice(0, 2 * tq), diag_rows=0):
        s = s_ref[rows, :]
        if diag_rows:
            row = lax.broadcasted_iota(jnp.int32, (diag_rows, tk), 0) % tk
            col = lax.broadcasted_iota(jnp.int32, (diag_rows, tk), 1)
            masked = jnp.where(col <= row, s[0:diag_rows], -jnp.inf)
            s = masked if diag_rows == s.shape[0] else jnp.concatenate([masked, s[diag_rows:]], axis=0)
        m_prev = m_s[rows, :]
        m_new = jnp.maximum(m_prev, jnp.max(s, axis=1, keepdims=True))
        alpha = jnp.exp2(m_prev - m_new)
        p = jnp.exp2(s - wide(m_new)).astype(BF16)
        r0 = pl.multiple_of(c * tk, tk)
        v1 = jnp.concatenate([v_ref[pl.ds(r0, tk), :], ones], axis=1)
        acc_s[rows, :] = jnp.concatenate([alpha, alpha], axis=1) * acc_s[rows, :] + _dot(p, v1)
        m_s[rows, :] = m_new

    scores(0, sa_s)
    before_loop()

    def body(j, carry):
        c0 = 2 * j
        scores(c0 + 1, sb_s)
        softmax_pv(c0, sa_s)
        scores(c0 + 2, sa_s)
        softmax_pv(c0 + 1, sb_s)
        return carry

    lax.fori_loop(0, qi, body, 0)
    c0 = 2 * qi
    late = slice(tq, 2 * tq)
    scores(c0 + 1, sb_s, late)
    softmax_pv(c0, sa_s, diag_rows=tq)
    softmax_pv(c0 + 1, sb_s, late, diag_rows=tq)
    after_loop()
    acc = acc_s[...]
    lam, gain = lam_ref[0, 0], gain_ref[...]
    for half in range(2):
        a1 = acc[2 * half * tk:(2 * half + 1) * tk]
        a2 = acc[(2 * half + 1) * tk:(2 * half + 2) * tk]
        o_ref[half * tk:(half + 1) * tk, :] = _subln(a1[:, 0:QK_DIM], a1[:, QK_DIM:2 * QK_DIM], a2[:, 0:QK_DIM],
                                                     a2[:, QK_DIM:2 * QK_DIM], lam, gain).astype(BF16)


def _decode_query(q_ref):
    q8 = q_ref[0].astype(F32)
    lane = lax.broadcasted_iota(jnp.int32, (N_HEADS, QK_DIM), 1)
    qz32 = jnp.concatenate([jnp.where(lane < HEAD_DIM, q8, 0.0), jnp.where(lane >= HEAD_DIM, q8, 0.0)], axis=0)
    return qz32, qz32.astype(BF16)


def _decode_group(q_ref, k_refs, v_refs):
    n_maps = 2 * N_HEADS
    rows_pp = PAGE_SIZE * N_HEADS
    _, qz = _decode_query(q_ref)
    n = len(k_refs)
    col = lax.broadcasted_iota(jnp.int32, (n_maps, n * rows_pp), 1)
    row = lax.broadcasted_iota(jnp.int32, (n_maps, n * rows_pp), 0)
    own_head = col % N_HEADS == row % N_HEADS
    s = jnp.concatenate([_dot_nt(qz, r[...].astype(BF16)) for r in k_refs], axis=1)
    s = jnp.where(own_head, s, -jnp.inf)
    m = jnp.max(s, axis=1, keepdims=True)
    p = jnp.exp2(s - m)
    pb = p.astype(BF16)
    pv = _dot(pb[:, 0:rows_pp], v_refs[0][...].astype(BF16))
    for i in range(1, n):
        pv = pv + _dot(pb[:, i * rows_pp:(i + 1) * rows_pp], v_refs[i][...].astype(BF16))
    return m, jnp.sum(p, axis=1, keepdims=True), pv


def _decode_finish(groups, lam_ref, gain_ref, q_ref, kn_ref, vn_ref, o_ref):
    n_maps = 2 * N_HEADS
    qz32, _ = _decode_query(q_ref)
    two = lambda a: jnp.concatenate([a, a], axis=0)
    s_n = jnp.sum(qz32 * two(kn_ref[0].astype(F32)), axis=1, keepdims=True)
    m = s_n
    for mg, _, _ in groups:
        m = jnp.maximum(m, mg)
    l = jnp.exp2(s_n - m)
    acc = l * two(vn_ref[0].astype(F32))
    for mg, lg, pvg in groups:
        w = jnp.exp2(mg - m)
        l = l + w * lg
        acc = acc + w * pvg
    o_ref[0] = _subln(acc[0:N_HEADS], l[0:N_HEADS], acc[N_HEADS:n_maps], l[N_HEADS:n_maps], lam_ref[0, 0],
                      gain_ref[...]).astype(BF16)


def _attn_kernel(pt_ref, lam_ref, q_ref, k_ref, v_ref, gain_ref, qs_ref, *refs, tq, tk, n_pages):
    del pt_ref
    k_refs, v_refs = refs[0:n_pages], refs[n_pages:2 * n_pages]
    kn_ref, vn_ref, o_ref, os_ref, sa_s, sb_s, m_s, acc_s = refs[2 * n_pages:]
    half = n_pages // 2
    groups = []

    def before_loop():
        groups.append(_decode_group(qs_ref, k_refs[0:half], v_refs[0:half]))

    def after_loop():
        groups.append(_decode_group(qs_ref, k_refs[half:], v_refs[half:]))
        _decode_finish(groups, lam_ref, gain_ref, qs_ref, kn_ref, vn_ref, os_ref)

    _prompt_tile(pl.program_id(2), lam_ref, q_ref, k_ref, v_ref, gain_ref, o_ref, sa_s, sb_s, m_s, acc_s, tq, tk,
                 before_loop, after_loop)


def _attention(page_table, lam, q, k, v, gain, q_s, cache_k, cache_v, k_new, v_new, nb, t_len, tq):
    tk = tq // 2
    n, n_pages = page_table.shape
    nq = t_len // tq
    assert nb * N_HEADS * nq == n, "one decode sample per prompt grid step"
    kern = functools.partial(_attn_kernel, tq=tq, tk=tk, n_pages=n_pages)
    sample = lambda b, h, i: (b * N_HEADS + h) * nq + i
    qspec = pl.BlockSpec((None, tq, QK_DIM), lambda b, h, i, pt: (b, i, h))
    kvspec = pl.BlockSpec((None, t_len, QK_DIM), lambda b, h, i, pt: (b, 0, h))
    row = pl.BlockSpec((1, N_HEADS, QK_DIM), lambda b, h, i, pt: (sample(b, h, i), 0, 0))

    def page(j):
        return pl.BlockSpec((None, PAGE_SIZE * N_HEADS, QK_DIM), lambda b, h, i, pt: (pt[sample(b, h, i), j], 0, 0))

    pages = [page(j) for j in range(n_pages)]
    grid_spec = pltpu.PrefetchScalarGridSpec(
        num_scalar_prefetch=1, grid=(nb, N_HEADS, nq),
        in_specs=[pl.BlockSpec(memory_space=pltpu.SMEM), qspec, kvspec, kvspec,
                  pl.BlockSpec((1, QK_DIM), lambda b, h, i, pt: (0, 0)), row] + pages + pages + [row, row],
        out_specs=(qspec, row),
        scratch_shapes=[pltpu.VMEM((2 * tq, tk), F32), pltpu.VMEM((2 * tq, tk), F32),
                        pltpu.VMEM((2 * tq, LANES), F32), pltpu.VMEM((2 * tq, 2 * QK_DIM), F32)])
    r3 = lambda a: a.reshape(n, N_HEADS, QK_DIM)
    out_shape = (jax.ShapeDtypeStruct((nb, t_len, ATTN_WIDTH), BF16), jax.ShapeDtypeStruct((n, N_HEADS, QK_DIM), BF16))
    return pl.pallas_call(
        kern, grid_spec=grid_spec, out_shape=out_shape,
        compiler_params=_params(("arbitrary", "arbitrary", "arbitrary"), ATTN_VMEM_LIMIT), name="attention")(
            page_table, lam, q, k, v, gain, r3(q_s), *([cache_k] * n_pages), *([cache_v] * n_pages),
            r3(k_new), r3(v_new))


FF_CHUNKS = ((0, 1024), (1024, 2048), (2048, D_FF))


def _out_ffn_kernel(x_ref, ys_ref, o_ref, gs_ref, ga_ref, wo_ref, npm_ref, npf_ref, wg_ref, wu_ref, wd_ref, nof_ref,
                    y_ref):
    mixed = gs_ref[...].astype(F32) * ys_ref[...].astype(F32) + ga_ref[...].astype(F32) * o_ref[...].astype(F32)
    x1 = x_ref[...] + _rms(_dot(mixed.astype(BF16), wo_ref[...]), npm_ref[...])
    hf = _rms(x1, npf_ref[...]).astype(BF16)
    f = None
    for c0, c1 in FF_CHUNKS:
        act = (jax.nn.silu(_dot(hf, wg_ref[:, c0:c1])) * _dot(hf, wu_ref[:, c0:c1])).astype(BF16)
        part = _dot(act, wd_ref[c0:c1, :])
        f = part if f is None else f + part
    y_ref[...] = x1 + _rms(f, nof_ref[...])


def _out_ffn(x2d, ys, o, gs, ga, wo, npm, npf, wg, wu, wd, nof, tm):
    m = x2d.shape[0]
    row = pl.BlockSpec((tm, D_MODEL), lambda i: (i, 0))
    vec = _resident((1, D_MODEL))
    return pl.pallas_call(
        _out_ffn_kernel, grid=(m // tm,),
        in_specs=[row, row, row, row, row, _resident((D_MODEL, D_MODEL)), vec, vec,
                  _resident((D_MODEL, D_FF)), _resident((D_MODEL, D_FF)), _resident((D_FF, D_MODEL)), vec],
        out_specs=row, out_shape=jax.ShapeDtypeStruct((m, D_MODEL), F32),
        compiler_params=_params(("parallel",)), name="out_ffn")(x2d, ys, o, gs, ga, wo, npm, npf, wg, wu, wd, nof)


def kernel(x_prompt, x_sample, cache_k, cache_v, state_ssm_re, state_ssm_im, page_table, norm_pre_mix, w_in,
           ssm_lambda_re, ssm_lambda_im, ssm_log_dt, ssm_b_re, ssm_b_im, ssm_c_re, ssm_c_im, ssm_d, w_glu_a,
           w_glu_b, lambda_q1, lambda_k1, lambda_q2, lambda_k2, subln_gain, w_o, norm_post_mix, norm_pre_ffn,
           w_gate, w_up, w_down, norm_post_ffn):
    assert w_in.shape[0] == 1, "single-layer stack"
    bp, tp, _ = x_prompt.shape
    bs, ts, _ = x_sample.shape
    assert ts == 1

    are, aim, bbre, bbim, lam = _prep(ssm_lambda_re[0], ssm_lambda_im[0], ssm_log_dt[0], ssm_b_re[0], ssm_b_im[0],
                                      lambda_q1[0], lambda_k1[0], lambda_q2[0], lambda_k2[0])
    are, aim = are.reshape(1, SSM_LANES), aim.reshape(1, SSM_LANES)
    bd = jnp.concatenate([_block_diag(bbre), _block_diag(bbim)], axis=1).astype(BF16)
    cre = _block_diag(jnp.swapaxes(ssm_c_re[0], 1, 2)).astype(BF16)
    cim = _block_diag(jnp.swapaxes(-ssm_c_im[0], 1, 2)).astype(BF16)
    dsk = ssm_d[0].reshape(1, SSM_WIDTH)
    wglu = jnp.concatenate([w_glu_a[0], w_glu_b[0]], axis=1).astype(BF16)
    vec = lambda a: a[0].reshape(1, -1)
    w_in_bf, wo_bf = w_in[0].astype(BF16), w_o[0].astype(BF16)
    wg_bf, wu_bf, wd_bf = w_gate[0].astype(BF16), w_up[0].astype(BF16), w_down[0].astype(BF16)
    gain = vec(subln_gain)
    ssm_w = (are, aim, bd, cre, cim, dsk, wglu)
    ffn_w = (wo_bf, vec(norm_post_mix), vec(norm_pre_ffn), wg_bf, wu_bf, wd_bf, vec(norm_post_ffn))

    xp = x_prompt.reshape(bp * tp, D_MODEL)
    u_il, q, kf, kb, vf, vb, gs, ga = _in_proj(xp, np.arange(tp), bp, vec(norm_pre_mix), w_in_bf, tm=512)
    ys, hpr, hpi = _ssm_prompt(u_il, bp, tp, *ssm_w, lc=128)

    xs = x_sample.reshape(bs, D_MODEL)
    pos_s = np.full((bs,), PAST_LEN)
    u_s, q_s, kf_s, kb_s, vf_s, vb_s, gs_s, ga_s = _in_proj(xs, pos_s, 1, vec(norm_pre_mix), w_in_bf, tm=bs)
    ys_s, hsr, hsi = _ssm_step(u_s, state_ssm_re[0].reshape(bs, SSM_LANES), state_ssm_im[0].reshape(bs, SSM_LANES),
                               *ssm_w)

    n_phys = cache_k.shape[1]
    pages = lambda c: c[0].reshape(n_phys, PAGE_SIZE * N_HEADS, QK_DIM)
    r3 = lambda a: a.reshape(bp, tp, ATTN_WIDTH)
    o, o_s = _attention(page_table, lam, r3(q), r3(kb), r3(vb), gain, q_s, pages(cache_k), pages(cache_v),
                        kb_s, vb_s, bp, tp, tq=1024)

    yp = _out_ffn(xp, ys.reshape(bp * tp, D_MODEL), o.reshape(bp * tp, ATTN_WIDTH), gs, ga, *ffn_w, tm=512)
    ysamp = _out_ffn(xs, ys_s, o_s.reshape(bs, ATTN_WIDTH), gs_s, ga_s, *ffn_w, tm=bs)

    st = lambda a, n: a.reshape(1, n, SSM_GROUPS, SSM_STATE)
    kv = lambda a, n, t: a.reshape(1, n, t, N_HEADS, QK_DIM)
    return (yp.reshape(bp, tp, D_MODEL), ysamp.reshape(bs, ts, D_MODEL),
            kv(kf, bp, tp), kv(vf, bp, tp), st(hpr, bp), st(hpi, bp),
            kv(kf_s, bs, ts), kv(vf_s, bs, ts), st(hsr, bs), st(hsi, bs))
```

```python
import functools
import math

import jax
import jax.numpy as jnp
import numpy as np
from jax import lax
from jax.experimental import pallas as pl
from jax.experimental.pallas import tpu as pltpu

F32 = jnp.float32
BF16 = jnp.bfloat16

D_MODEL = 1024
PAST_LEN = 2048
PAGE_SIZE = 128
SSM_WIDTH = 512
SSM_GROUP = 16
SSM_GROUPS = 32
SSM_STATE = 64
SSM_LANES = SSM_GROUPS * SSM_STATE
HEAD_DIM = 64
N_HEADS = 8
QK_DIM = 128
ATTN_WIDTH = 1024
ROPE_DIM = 16
ROPE_THETA = 500000.0
D_FF = 2816
IN_WIDTH = SSM_WIDTH + 3 * ATTN_WIDTH + 2 * D_MODEL
EPS = 1e-6
LAM_INIT = 0.8 - 0.6 * math.exp(-0.3 * 0)
Q_SCALE = HEAD_DIM ** -0.5 * math.log2(math.e)

LANES = 128
SUBLANES = 8
VMEM_LIMIT = 56 * 1024 * 1024
ATTN_VMEM_LIMIT = 62 * 1024 * 1024


ROW_SUBTILES = 2
MIN_SUBTILE_ROWS = 256


def _params(sem, vmem=VMEM_LIMIT):
    return pltpu.CompilerParams(dimension_semantics=sem, vmem_limit_bytes=vmem)


def _row_subtiles(tm):
    sub = max(tm // ROW_SUBTILES, min(tm, MIN_SUBTILE_ROWS))
    return [slice(r0, r0 + sub) for r0 in range(0, tm, sub)]


def _resident(shape):
    return pl.BlockSpec(shape, lambda *_: (0,) * len(shape), pipeline_mode=pl.Buffered(1))


def _rms(x, g):
    return x * lax.rsqrt(jnp.mean(x * x, axis=-1, keepdims=True) + EPS) * g


def _dot(a, b):
    return jnp.dot(a, b, preferred_element_type=F32)


def _dot_nt(a, b):
    return lax.dot_general(a, b, (((1,), (1,)), ((), ())), preferred_element_type=F32)


def _prep_kernel(lre_ref, lim_ref, ldt_ref, bre_ref, bim_ref, lq1_ref, lk1_ref, lq2_ref, lk2_ref,
                 are_ref, aim_ref, bbre_ref, bbim_ref, lam_ref):
    lre, lim = lre_ref[...], lim_ref[...]
    dt = jnp.exp(ldt_ref[...])
    mag = jnp.exp(lre * dt)
    ar, ai = mag * jnp.cos(lim * dt), mag * jnp.sin(lim * dt)
    den = lre * lre + lim * lim
    fr = ((ar - 1.0) * lre + ai * lim) / den
    fi = (ai * lre - (ar - 1.0) * lim) / den
    are_ref[...] = ar
    aim_ref[...] = ai
    bre, bim = bre_ref[...], bim_ref[...]
    bbre_ref[...] = fr[:, None, :] * bre - fi[:, None, :] * bim
    bbim_ref[...] = fr[:, None, :] * bim + fi[:, None, :] * bre
    s1 = jnp.sum(lq1_ref[...] * lk1_ref[...], axis=-1, keepdims=True)
    s2 = jnp.sum(lq2_ref[...] * lk2_ref[...], axis=-1, keepdims=True)
    lam_ref[...] = jnp.exp(s1) - jnp.exp(s2) + LAM_INIT


def _prep(lam_re, lam_im, log_dt, b_re, b_im, lq1, lk1, lq2, lk2):
    g, p, c = SSM_GROUPS, SSM_STATE, SSM_GROUP
    out_shape = (jax.ShapeDtypeStruct((g, p), F32), jax.ShapeDtypeStruct((g, p), F32),
                 jax.ShapeDtypeStruct((g, c, p), F32), jax.ShapeDtypeStruct((g, c, p), F32),
                 jax.ShapeDtypeStruct((1, 1), F32))
    return pl.pallas_call(_prep_kernel, out_shape=out_shape, name="ssm_prep")(
        lam_re, lam_im, log_dt.reshape(g, 1), jnp.swapaxes(b_re, 1, 2), jnp.swapaxes(b_im, 1, 2),
        lq1.reshape(1, HEAD_DIM), lk1.reshape(1, HEAD_DIM), lq2.reshape(1, HEAD_DIM), lk2.reshape(1, HEAD_DIM))


def _block_diag(blocks):
    g, r, c = blocks.shape
    keep = np.arange(g * r)[:, None] // r == np.arange(g * c)[None, :] // c
    return jnp.where(keep, jnp.tile(blocks.reshape(g * r, c), (1, g)), 0.0)


def _in_proj_kernel(x_ref, g_ref, w_ref, cos_ref, sa_ref, sb_ref,
                    u_ref, q_ref, kf_ref, kb_ref, vf_ref, vb_ref, gs_ref, ga_ref):
    o0 = SSM_WIDTH
    o1, o2, o3, o4 = o0 + ATTN_WIDTH, o0 + 2 * ATTN_WIDTH, o0 + 3 * ATTN_WIDTH, o0 + 3 * ATTN_WIDTH + D_MODEL
    subs = _row_subtiles(x_ref.shape[0])
    sub = subs[0].stop
    heads = [slice(hh * QK_DIM, (hh + 1) * QK_DIM) for hh in range(N_HEADS)]

    def rope(z, rows):
        return (z * cos_ref[rows, :] + pltpu.roll(z, LANES - ROPE_DIM // 2, 1) * sa_ref[rows, :]
                + pltpu.roll(z, ROPE_DIM // 2, 1) * sb_ref[rows, :])

    def seg_u(rows, h):
        u_ref[rows, :] = _dot(h, w_ref[:, 0:o0]).astype(BF16)

    def seg_q(rows, h):
        zq = _dot(h, w_ref[:, o0:o1])
        for sl in heads:
            q_ref[rows, sl] = (rope(zq[:, sl], rows) * Q_SCALE).astype(BF16)

    def seg_k(rows, h):
        zk = _dot(h, w_ref[:, o1:o2])
        for hh, sl in enumerate(heads):
            kr = rope(zk[:, sl], rows)
            kf_ref[pl.ds(rows.start * N_HEADS + hh, sub, stride=N_HEADS), :] = kr
            kb_ref[rows, sl] = kr.astype(BF16)

    def seg_v(rows, h):
        zv = _dot(h, w_ref[:, o2:o3])
        for hh, sl in enumerate(heads):
            vf_ref[pl.ds(rows.start * N_HEADS + hh, sub, stride=N_HEADS), :] = zv[:, sl]
        vb_ref[rows, :] = zv.astype(BF16)

    def seg_gs(rows, h):
        gs_ref[rows, :] = jax.nn.sigmoid(_dot(h, w_ref[:, o3:o4])).astype(BF16)

    def seg_ga(rows, h):
        ga_ref[rows, :] = jax.nn.sigmoid(_dot(h, w_ref[:, o4:IN_WIDTH])).astype(BF16)

    hs = [_rms(x_ref[rows, :], g_ref[...]).astype(BF16) for rows in subs]
    for seg in (seg_u, seg_q, seg_k, seg_v, seg_gs, seg_ga):
        for rows, h in zip(subs, hs):
            seg(rows, h)


def _rope_tables(pos):
    half = ROPE_DIM // 2
    inv = ROPE_THETA ** (-np.arange(half, dtype=np.float64) * 2.0 / ROPE_DIM)
    ang = np.asarray(pos, np.float64)[:, None] * inv[None, :]
    cos, sin = np.cos(ang), np.sin(ang)
    n = ang.shape[0]
    comp_cos = np.concatenate([cos, cos, np.ones((n, HEAD_DIM - ROPE_DIM))], axis=1)
    comp_sa = np.concatenate([-sin, np.zeros((n, HEAD_DIM - half))], axis=1)
    comp_sb = np.concatenate([np.zeros((n, half)), sin, np.zeros((n, HEAD_DIM - ROPE_DIM))], axis=1)
    two = lambda t: jnp.asarray(np.concatenate([t, t], axis=1), F32)
    return two(comp_cos), two(comp_sa), two(comp_sb)


def _in_proj(x2d, pos, nb, g_pre, w_in_bf, tm):
    m = x2d.shape[0]
    t_len = m // nb
    nt = t_len // tm
    cos, sa, sb = _rope_tables(pos)
    row = lambda w: pl.BlockSpec((tm, w), lambda i: (i, 0))
    heads = pl.BlockSpec((tm * N_HEADS, QK_DIM), lambda i: (i, 0))
    tab = pl.BlockSpec((tm, LANES), lambda i: (i % nt, 0))
    out_shape = (jax.ShapeDtypeStruct((t_len, nb * SSM_WIDTH), BF16),
                 jax.ShapeDtypeStruct((m, ATTN_WIDTH), BF16),
                 jax.ShapeDtypeStruct((m * N_HEADS, QK_DIM), F32), jax.ShapeDtypeStruct((m, ATTN_WIDTH), BF16),
                 jax.ShapeDtypeStruct((m * N_HEADS, QK_DIM), F32), jax.ShapeDtypeStruct((m, ATTN_WIDTH), BF16),
                 jax.ShapeDtypeStruct((m, D_MODEL), BF16), jax.ShapeDtypeStruct((m, D_MODEL), BF16))
    out_specs = (pl.BlockSpec((tm, SSM_WIDTH), lambda i: (i % nt, i // nt)),
                 row(ATTN_WIDTH), heads, row(ATTN_WIDTH), heads, row(ATTN_WIDTH),
                 row(D_MODEL), row(D_MODEL))
    return pl.pallas_call(
        _in_proj_kernel, grid=(m // tm,),
        in_specs=[row(D_MODEL), _resident((1, D_MODEL)), _resident((D_MODEL, IN_WIDTH)), tab, tab, tab],
        out_specs=out_specs, out_shape=out_shape,
        compiler_params=_params(("parallel",)), name="in_proj")(x2d, g_pre, w_in_bf, cos, sa, sb)


def _ssm_drive(u, bd_ref, store):
    n_tiles = 2 * SSM_LANES // 256
    for j in range(n_tiles):
        ks = LANES * ((j % (n_tiles // 2)) // 2)
        lanes = slice(256 * j, 256 * (j + 1))
        store(lanes, _dot(u[:, ks:ks + LANES], bd_ref[ks:ks + LANES, lanes]))


def _ssm_readout(load, u, cre_ref, cim_ref, dsk_ref):
    half = SSM_LANES // 2
    ys = []
    for j in range(2):
        hre = load(slice(half * j, half * (j + 1))).astype(BF16)
        him = load(slice(SSM_LANES + half * j, SSM_LANES + half * (j + 1))).astype(BF16)
        cols = slice(256 * j, 256 * (j + 1))
        ys.append(_dot(hre, cre_ref[half * j:half * (j + 1), cols]) + _dot(him, cim_ref[half * j:half * (j + 1), cols]))
    y = jnp.concatenate(ys, axis=1) + dsk_ref[...] * u.astype(F32)
    return jax.nn.gelu(y)


def _glu(yg, wglu_ref):
    return _dot(yg, wglu_ref[:, 0:D_MODEL]) * jax.nn.sigmoid(_dot(yg, wglu_ref[:, D_MODEL:2 * D_MODEL]))


def _ssm_prompt_kernel(u_ref, are_ref, aim_ref, bd_ref, cre_ref, cim_ref, dsk_ref, wglu_ref,
                       ys_ref, hre_ref, him_ref, h_s, il_s, st_re, st_im, *, nb, lc):
    @pl.when(pl.program_id(0) == 0)
    def _():
        st_re[...] = jnp.zeros_like(st_re)
        st_im[...] = jnp.zeros_like(st_im)

    rows = lc * nb
    n_tr = rows // SUBLANES
    n_slabs = SSM_WIDTH // LANES
    slab = lambda j: slice(j * LANES, (j + 1) * LANES)
    seq_rows = lambda b: pl.ds(b, lc, stride=nb)
    for b in range(nb):
        ub = u_ref[:, b * SSM_WIDTH:(b + 1) * SSM_WIDTH].astype(F32)
        for j in range(n_slabs):
            il_s[j, seq_rows(b), :] = ub[:, slab(j)]
    u = jnp.concatenate([il_s[j] for j in range(n_slabs)], axis=1).astype(BF16)

    def store(lanes, val):
        h_s[:, :, lanes] = val.reshape(n_tr, SUBLANES, val.shape[-1])

    _ssm_drive(u, bd_ref, store)

    def step(s, carry):
        hr, hi = carry
        ar, ai = are_ref[...], aim_ref[...]
        for k in range(SUBLANES // nb):
            sub = slice(k * nb, (k + 1) * nb)
            nhr = ar * hr - ai * hi + h_s[s, sub, 0:SSM_LANES]
            nhi = ar * hi + ai * hr + h_s[s, sub, SSM_LANES:2 * SSM_LANES]
            h_s[s, sub, 0:SSM_LANES] = nhr
            h_s[s, sub, SSM_LANES:2 * SSM_LANES] = nhi
            hr, hi = nhr, nhi
        return hr, hi

    hr, hi = lax.fori_loop(0, n_tr, step, (st_re[...], st_im[...]))
    st_re[...] = hr
    st_im[...] = hi
    hre_ref[...] = hr
    him_ref[...] = hi

    load = lambda lanes: h_s[:, :, lanes].reshape(rows, lanes.stop - lanes.start)
    yg = _ssm_readout(load, u, cre_ref, cim_ref, dsk_ref)
    for j in range(n_slabs):
        il_s[j] = yg[:, slab(j)]
    per_seq = [jnp.concatenate([il_s[j, seq_rows(b), :] for j in range(n_slabs)], axis=1) for b in range(nb)]
    res = _glu(jnp.concatenate(per_seq, axis=0).astype(BF16), wglu_ref)
    for b in range(nb):
        ys_ref[b] = res[b * lc:(b + 1) * lc].astype(BF16)


def _ssm_prompt(u_il, nb, t_len, are, aim, bd, cre, cim, dsk, wglu, lc):
    rows = lc * nb
    kern = functools.partial(_ssm_prompt_kernel, nb=nb, lc=lc)
    out_shape = (jax.ShapeDtypeStruct((nb, t_len, D_MODEL), BF16),
                 jax.ShapeDtypeStruct((nb, SSM_LANES), F32), jax.ShapeDtypeStruct((nb, SSM_LANES), F32))
    return pl.pallas_call(
        kern, grid=(t_len // lc,),
        in_specs=[pl.BlockSpec((lc, nb * SSM_WIDTH), lambda c: (c, 0)),
                  _resident((nb, SSM_LANES)), _resident((nb, SSM_LANES)),
                  _resident((SSM_WIDTH, 2 * SSM_LANES)), _resident((SSM_LANES, SSM_WIDTH)),
                  _resident((SSM_LANES, SSM_WIDTH)), _resident((1, SSM_WIDTH)),
                  _resident((SSM_WIDTH, 2 * D_MODEL))],
        out_specs=(pl.BlockSpec((nb, lc, D_MODEL), lambda c: (0, c, 0)),
                   pl.BlockSpec((nb, SSM_LANES), lambda c: (0, 0)), pl.BlockSpec((nb, SSM_LANES), lambda c: (0, 0))),
        out_shape=out_shape,
        scratch_shapes=[pltpu.VMEM((rows // SUBLANES, SUBLANES, 2 * SSM_LANES), F32),
                        pltpu.VMEM((SSM_WIDTH // LANES, rows, LANES), F32),
                        pltpu.VMEM((nb, SSM_LANES), F32), pltpu.VMEM((nb, SSM_LANES), F32)],
        compiler_params=_params(("arbitrary",)), name="ssm_prompt")(
            u_il, jnp.tile(are, (nb, 1)), jnp.tile(aim, (nb, 1)), bd, cre, cim, dsk, wglu)


def _ssm_step_kernel(u_ref, h0re_ref, h0im_ref, are_ref, aim_ref, bd_ref, cre_ref, cim_ref, dsk_ref, wglu_ref,
                     ys_ref, hre_ref, him_ref, h_s):
    u = u_ref[...]

    def store(lanes, val):
        h_s[:, lanes] = val

    _ssm_drive(u, bd_ref, store)
    ar, ai = are_ref[...], aim_ref[...]
    hr, hi = h0re_ref[...], h0im_ref[...]
    nhr = ar * hr - ai * hi + h_s[:, 0:SSM_LANES]
    nhi = ar * hi + ai * hr + h_s[:, SSM_LANES:2 * SSM_LANES]
    hre_ref[...] = nhr
    him_ref[...] = nhi
    h_s[:, 0:SSM_LANES] = nhr
    h_s[:, SSM_LANES:2 * SSM_LANES] = nhi
    yg = _ssm_readout(lambda lanes: h_s[:, lanes], u, cre_ref, cim_ref, dsk_ref)
    ys_ref[...] = _glu(yg.astype(BF16), wglu_ref).astype(BF16)


def _ssm_step(u, h0re, h0im, are, aim, bd, cre, cim, dsk, wglu):
    n = u.shape[0]
    out_shape = (jax.ShapeDtypeStruct((n, D_MODEL), BF16),
                 jax.ShapeDtypeStruct((n, SSM_LANES), F32), jax.ShapeDtypeStruct((n, SSM_LANES), F32))
    return pl.pallas_call(
        _ssm_step_kernel, out_shape=out_shape,
        scratch_shapes=[pltpu.VMEM((n, 2 * SSM_LANES), F32)],
        compiler_params=pltpu.CompilerParams(vmem_limit_bytes=VMEM_LIMIT), name="ssm_step")(
            u, h0re, h0im, are, aim, bd, cre, cim, dsk, wglu)


def _subln(o1, l1, o2, l2, lam, gain):
    o = o1 / l1 - lam * (o2 / l2)
    return _rms(o, gain) * (1.0 - LAM_INIT)


def _prompt_tile(qi, lam_ref, q_ref, k_ref, v_ref, gain_ref, o_ref, sa_s, sb_s, m_s, acc_s, tq, tk,
                 before_loop, after_loop):
    lane = lax.broadcasted_iota(jnp.int32, (tk, QK_DIM), 1)
    parts = []
    for half in range(2):
        q = q_ref[half * tk:(half + 1) * tk, :]
        zero = jnp.zeros_like(q)
        parts += [jnp.where(lane < HEAD_DIM, q, zero), jnp.where(lane >= HEAD_DIM, q, zero)]
    qz = jnp.concatenate(parts, axis=0)
    m_s[...] = jnp.full_like(m_s, -jnp.inf)
    acc_s[...] = jnp.zeros_like(acc_s)
    ones = jnp.ones((tk, LANES), BF16)
    wide = lambda a: jnp.concatenate([a] * (tk // LANES), axis=1)

    def scores(c, s_ref, rows=slice(0, 2 * tq)):
        r0 = pl.multiple_of(c * tk, tk)
        s_ref[rows, :] = _dot_nt(qz[rows], k_ref[pl.ds(r0, tk), :])

    def softmax_pv(c, s_ref, rows=slice(0, 2 * tq), diag_rows=0):
        s = s_ref[rows, :]
        if diag_rows:
            row = lax.broadcasted_iota(jnp.int32, (diag_rows, tk), 0) % tk
            col = lax.broadcasted_iota(jnp.int32, (diag_rows, tk), 1)
            masked = jnp.where(col <= row, s[0:diag_rows], -jnp.inf)
            s = masked if diag_rows == s.shape[0] else jnp.concatenate([masked, s[diag_rows:]], axis=0)
        m_prev = m_s[rows, :]
        m_new = jnp.maximum(m_prev, jnp.max(s, axis=1, keepdims=True))
        alpha = jnp.exp2(m_prev - m_new)
        p = jnp.exp2(s - wide(m_new)).astype(BF16)
        r0 = pl.multiple_of(c * tk, tk)
        v1 = jnp.concatenate([v_ref[pl.ds(r0, tk), :], ones], axis=1)
        acc_s[rows, :] = jnp.concatenate([alpha, alpha], axis=1) * acc_s[rows, :] + _dot(p, v1)
        m_s[rows, :] = m_new

    scores(0, sa_s)
    before_loop()

    def body(j, carry):
        c0 = 2 * j
        scores(c0 + 1, sb_s)
        softmax_pv(c0, sa_s)
        scores(c0 + 2, sa_s)
        softmax_pv(c0 + 1, sb_s)
        return carry

    lax.fori_loop(0, qi, body, 0)
    c0 = 2 * qi
    late = slice(tq, 2 * tq)
    scores(c0 + 1, sb_s, late)
    softmax_pv(c0, sa_s, diag_rows=tq)
    softmax_pv(c0 + 1, sb_s, late, diag_rows=tq)
    after_loop()
    acc = acc_s[...]
    lam, gain = lam_ref[0, 0], gain_ref[...]
    for half in range(2):
        a1 = acc[2 * half * tk:(2 * half + 1) * tk]
        a2 = acc[(2 * half + 1) * tk:(2 * half + 2) * tk]
        o_ref[half * tk:(half + 1) * tk, :] = _subln(a1[:, 0:QK_DIM], a1[:, QK_DIM:2 * QK_DIM], a2[:, 0:QK_DIM],
                                                     a2[:, QK_DIM:2 * QK_DIM], lam, gain).astype(BF16)


def _decode_query(q_ref):
    q8 = q_ref[0].astype(F32)
    lane = lax.broadcasted_iota(jnp.int32, (N_HEADS, QK_DIM), 1)
    qz32 = jnp.concatenate([jnp.where(lane < HEAD_DIM, q8, 0.0), jnp.where(lane >= HEAD_DIM, q8, 0.0)], axis=0)
    return qz32, qz32.astype(BF16)


def _decode_group(q_ref, k_refs, v_refs):
    n_maps = 2 * N_HEADS
    rows_pp = PAGE_SIZE * N_HEADS
    _, qz = _decode_query(q_ref)
    n = len(k_refs)
    col = lax.broadcasted_iota(jnp.int32, (n_maps, n * rows_pp), 1)
    row = lax.broadcasted_iota(jnp.int32, (n_maps, n * rows_pp), 0)
    own_head = col % N_HEADS == row % N_HEADS
    s = jnp.concatenate([_dot_nt(qz, r[...].astype(BF16)) for r in k_refs], axis=1)
    s = jnp.where(own_head, s, -jnp.inf)
    m = jnp.max(s, axis=1, keepdims=True)
    p = jnp.exp2(s - m)
    pb = p.astype(BF16)
    pv = _dot(pb[:, 0:rows_pp], v_refs[0][...].astype(BF16))
    for i in range(1, n):
        pv = pv + _dot(pb[:, i * rows_pp:(i + 1) * rows_pp], v_refs[i][...].astype(BF16))
    return m, jnp.sum(p, axis=1, keepdims=True), pv


def _decode_finish(groups, lam_ref, gain_ref, q_ref, kn_ref, vn_ref, o_ref):
    n_maps = 2 * N_HEADS
    qz32, _ = _decode_query(q_ref)
    two = lambda a: jnp.concatenate([a, a], axis=0)
    s_n = jnp.sum(qz32 * two(kn_ref[0].astype(F32)), axis=1, keepdims=True)
    m = s_n
    for mg, _, _ in groups:
        m = jnp.maximum(m, mg)
    l = jnp.exp2(s_n - m)
    acc = l * two(vn_ref[0].astype(F32))
    for mg, lg, pvg in groups:
        w = jnp.exp2(mg - m)
        l = l + w * lg
        acc = acc + w * pvg
    o_ref[0] = _subln(acc[0:N_HEADS], l[0:N_HEADS], acc[N_HEADS:n_maps], l[N_HEADS:n_maps], lam_ref[0, 0],
                      gain_ref[...]).astype(BF16)


def _attn_kernel(pt_ref, lam_ref, q_ref, k_ref, v_ref, gain_ref, qs_ref, *refs, tq, tk, n_pages):
    del pt_ref
    k_refs, v_refs = refs[0:n_pages], refs[n_pages:2 * n_pages]
    kn_ref, vn_ref, o_ref, os_ref, sa_s, sb_s, m_s, acc_s = refs[2 * n_pages:]
    half = n_pages // 2
    groups = []

    def before_loop():
        groups.append(_decode_group(qs_ref, k_refs[0:half], v_refs[0:half]))

    def after_loop():
        groups.append(_decode_group(qs_ref, k_refs[half:], v_refs[half:]))
        _decode_finish(groups, lam_ref, gain_ref, qs_ref, kn_ref, vn_ref, os_ref)

    _prompt_tile(pl.program_id(2), lam_ref, q_ref, k_ref, v_ref, gain_ref, o_ref, sa_s, sb_s, m_s, acc_s, tq, tk,
                 before_loop, after_loop)


def _attention(page_table, lam, q, k, v, gain, q_s, cache_k, cache_v, k_new, v_new, nb, t_len, tq):
    tk = tq // 2
    n, n_pages = page_table.shape
    nq = t_len // tq
    assert nb * N_HEADS * nq == n, "one decode sample per prompt grid step"
    kern = functools.partial(_attn_kernel, tq=tq, tk=tk, n_pages=n_pages)
    sample = lambda b, h, i: (b * N_HEADS + h) * nq + i
    qspec = pl.BlockSpec((None, tq, QK_DIM), lambda b, h, i, pt: (b, i, h))
    kvspec = pl.BlockSpec((None, t_len, QK_DIM), lambda b, h, i, pt: (b, 0, h))
    row = pl.BlockSpec((1, N_HEADS, QK_DIM), lambda b, h, i, pt: (sample(b, h, i), 0, 0))

    def page(j):
        return pl.BlockSpec((None, PAGE_SIZE * N_HEADS, QK_DIM), lambda b, h, i, pt: (pt[sample(b, h, i), j], 0, 0))

    pages = [page(j) for j in range(n_pages)]
    grid_spec = pltpu.PrefetchScalarGridSpec(
        num_scalar_prefetch=1, grid=(nb, N_HEADS, nq),
        in_specs=[pl.BlockSpec(memory_space=pltpu.SMEM), qspec, kvspec, kvspec,
                  pl.BlockSpec((1, QK_DIM), lambda b, h, i, pt: (0, 0)), row] + pages + pages + [row, row],
        out_specs=(qspec, row),
        scratch_shapes=[pltpu.VMEM((2 * tq, tk), F32), pltpu.VMEM((2 * tq, tk), F32),
                        pltpu.VMEM((2 * tq, LANES), F32), pltpu.VMEM((2 * tq, 2 * QK_DIM), F32)])
    r3 = lambda a: a.reshape(n, N_HEADS, QK_DIM)
    out_shape = (jax.ShapeDtypeStruct((nb, t_len, ATTN_WIDTH), BF16), jax.ShapeDtypeStruct((n, N_HEADS, QK_DIM), BF16))
    return pl.pallas_call(
        kern, grid_spec=grid_spec, out_shape=out_shape,
        compiler_params=_params(("arbitrary", "arbitrary", "arbitrary"), ATTN_VMEM_LIMIT), name="attention")(
            page_table, lam, q, k, v, gain, r3(q_s), *([cache_k] * n_pages), *([cache_v] * n_pages),
            r3(k_new), r3(v_new))


FF_CHUNKS = ((0, 1024), (1024, 2048), (2048, D_FF))


def _out_ffn_kernel(x_ref, ys_ref, o_ref, gs_ref, ga_ref, wo_ref, npm_ref, npf_ref, wg_ref, wu_ref, wd_ref, nof_ref,
                    y_ref):
    subs = _row_subtiles(x_ref.shape[0])
    f32 = lambda ref, rows: ref[rows, :].astype(F32)
    mixed = [(f32(gs_ref, r) * f32(ys_ref, r) + f32(ga_ref, r) * f32(o_ref, r)).astype(BF16) for r in subs]
    proj = [_dot(m, wo_ref[...]) for m in mixed]
    x1 = [x_ref[r, :] + _rms(p, npm_ref[...]) for r, p in zip(subs, proj)]
    hf = [_rms(x, npf_ref[...]).astype(BF16) for x in x1]
    f = [None] * len(subs)
    for c0, c1 in FF_CHUNKS:
        for i, h in enumerate(hf):
            act = (jax.nn.silu(_dot(h, wg_ref[:, c0:c1])) * _dot(h, wu_ref[:, c0:c1])).astype(BF16)
            part = _dot(act, wd_ref[c0:c1, :])
            f[i] = part if f[i] is None else f[i] + part
    for r, x, fi in zip(subs, x1, f):
        y_ref[r, :] = x + _rms(fi, nof_ref[...])


def _out_ffn(x2d, ys, o, gs, ga, wo, npm, npf, wg, wu, wd, nof, tm):
    m = x2d.shape[0]
    row = pl.BlockSpec((tm, D_MODEL), lambda i: (i, 0))
    vec = _resident((1, D_MODEL))
    return pl.pallas_call(
        _out_ffn_kernel, grid=(m // tm,),
        in_specs=[row, row, row, row, row, _resident((D_MODEL, D_MODEL)), vec, vec,
                  _resident((D_MODEL, D_FF)), _resident((D_MODEL, D_FF)), _resident((D_FF, D_MODEL)), vec],
        out_specs=row, out_shape=jax.ShapeDtypeStruct((m, D_MODEL), F32),
        compiler_params=_params(("parallel",)), name="out_ffn")(x2d, ys, o, gs, ga, wo, npm, npf, wg, wu, wd, nof)


def kernel(x_prompt, x_sample, cache_k, cache_v, state_ssm_re, state_ssm_im, page_table, norm_pre_mix, w_in,
           ssm_lambda_re, ssm_lambda_im, ssm_log_dt, ssm_b_re, ssm_b_im, ssm_c_re, ssm_c_im, ssm_d, w_glu_a,
           w_glu_b, lambda_q1, lambda_k1, lambda_q2, lambda_k2, subln_gain, w_o, norm_post_mix, norm_pre_ffn,
           w_gate, w_up, w_down, norm_post_ffn):
    assert w_in.shape[0] == 1, "single-layer stack"
    bp, tp, _ = x_prompt.shape
    bs, ts, _ = x_sample.shape
    assert ts == 1

    are, aim, bbre, bbim, lam = _prep(ssm_lambda_re[0], ssm_lambda_im[0], ssm_log_dt[0], ssm_b_re[0], ssm_b_im[0],
                                      lambda_q1[0], lambda_k1[0], lambda_q2[0], lambda_k2[0])
    are, aim = are.reshape(1, SSM_LANES), aim.reshape(1, SSM_LANES)
    bd = jnp.concatenate([_block_diag(bbre), _block_diag(bbim)], axis=1).astype(BF16)
    cre = _block_diag(jnp.swapaxes(ssm_c_re[0], 1, 2)).astype(BF16)
    cim = _block_diag(jnp.swapaxes(-ssm_c_im[0], 1, 2)).astype(BF16)
    dsk = ssm_d[0].reshape(1, SSM_WIDTH)
    wglu = jnp.concatenate([w_glu_a[0], w_glu_b[0]], axis=1).astype(BF16)
    vec = lambda a: a[0].reshape(1, -1)
    w_in_bf, wo_bf = w_in[0].astype(BF16), w_o[0].astype(BF16)
    wg_bf, wu_bf, wd_bf = w_gate[0].astype(BF16), w_up[0].astype(BF16), w_down[0].astype(BF16)
    gain = vec(subln_gain)
    ssm_w = (are, aim, bd, cre, cim, dsk, wglu)
    ffn_w = (wo_bf, vec(norm_post_mix), vec(norm_pre_ffn), wg_bf, wu_bf, wd_bf, vec(norm_post_ffn))

    xp = x_prompt.reshape(bp * tp, D_MODEL)
    u_il, q, kf, kb, vf, vb, gs, ga = _in_proj(xp, np.arange(tp), bp, vec(norm_pre_mix), w_in_bf, tm=512)
    ys, hpr, hpi = _ssm_prompt(u_il, bp, tp, *ssm_w, lc=128)

    xs = x_sample.reshape(bs, D_MODEL)
    pos_s = np.full((bs,), PAST_LEN)
    u_s, q_s, kf_s, kb_s, vf_s, vb_s, gs_s, ga_s = _in_proj(xs, pos_s, 1, vec(norm_pre_mix), w_in_bf, tm=bs)
    ys_s, hsr, hsi = _ssm_step(u_s, state_ssm_re[0].reshape(bs, SSM_LANES), state_ssm_im[0].reshape(bs, SSM_LANES),
                               *ssm_w)

    n_phys = cache_k.shape[1]
    pages = lambda c: c[0].reshape(n_phys, PAGE_SIZE * N_HEADS, QK_DIM)
    r3 = lambda a: a.reshape(bp, tp, ATTN_WIDTH)
    o, o_s = _attention(page_table, lam, r3(q), r3(kb), r3(vb), gain, q_s, pages(cache_k), pages(cache_v),
                        kb_s, vb_s, bp, tp, tq=1024)

    yp = _out_ffn(xp, ys.reshape(bp * tp, D_MODEL), o.reshape(bp * tp, ATTN_WIDTH), gs, ga, *ffn_w, tm=512)
    ysamp = _out_ffn(xs, ys_s, o_s.reshape(bs, ATTN_WIDTH), gs_s, ga_s, *ffn_w, tm=bs)

    st = lambda a, n: a.reshape(1, n, SSM_GROUPS, SSM_STATE)
    kv = lambda a, n, t: a.reshape(1, n, t, N_HEADS, QK_DIM)
    return (yp.reshape(bp, tp, D_MODEL), ysamp.reshape(bs, ts, D_MODEL),
            kv(kf, bp, tp), kv(vf, bp, tp), st(hpr, bp), st(hpi, bp),
            kv(kf_s, bs, ts), kv(vf_s, bs, ts), st(hsr, bs), st(hsi, bs))
```

```python
import functools
import math

import jax
import jax.numpy as jnp
import numpy as np
from jax import lax
from jax.experimental import pallas as pl
from jax.experimental.pallas import tpu as pltpu

F32 = jnp.float32
BF16 = jnp.bfloat16

D_MODEL = 1024
PAST_LEN = 2048
PAGE_SIZE = 128
SSM_WIDTH = 512
SSM_GROUP = 16
SSM_GROUPS = 32
SSM_STATE = 64
SSM_LANES = SSM_GROUPS * SSM_STATE
HEAD_DIM = 64
N_HEADS = 8
QK_DIM = 128
ATTN_WIDTH = 1024
ROPE_DIM = 16
ROPE_THETA = 500000.0
D_FF = 2816
IN_WIDTH = SSM_WIDTH + 3 * ATTN_WIDTH + 2 * D_MODEL
EPS = 1e-6
LAM_INIT = 0.8 - 0.6 * math.exp(-0.3 * 0)
Q_SCALE = HEAD_DIM ** -0.5 * math.log2(math.e)

LANES = 128
SUBLANES = 8
VMEM_LIMIT = 56 * 1024 * 1024
ATTN_VMEM_LIMIT = 62 * 1024 * 1024
DECODE_GROUPS = 2


ROW_SUBTILES = 2
MIN_SUBTILE_ROWS = 256


def _params(sem, vmem=VMEM_LIMIT):
    return pltpu.CompilerParams(dimension_semantics=sem, vmem_limit_bytes=vmem)


def _row_subtiles(tm):
    sub = max(tm // ROW_SUBTILES, min(tm, MIN_SUBTILE_ROWS))
    return [slice(r0, r0 + sub) for r0 in range(0, tm, sub)]


def _resident(shape):
    return pl.BlockSpec(shape, lambda *_: (0,) * len(shape), pipeline_mode=pl.Buffered(1))


def _rms(x, g):
    return x * lax.rsqrt(jnp.mean(x * x, axis=-1, keepdims=True) + EPS) * g


def _dot(a, b):
    return jnp.dot(a, b, preferred_element_type=F32)


def _dot_nt(a, b):
    return lax.dot_general(a, b, (((1,), (1,)), ((), ())), preferred_element_type=F32)


def _prep_kernel(lre_ref, lim_ref, ldt_ref, bre_ref, bim_ref, lq1_ref, lk1_ref, lq2_ref, lk2_ref,
                 are_ref, aim_ref, bbre_ref, bbim_ref, lam_ref):
    lre, lim = lre_ref[...], lim_ref[...]
    dt = jnp.exp(ldt_ref[...])
    mag = jnp.exp(lre * dt)
    ar, ai = mag * jnp.cos(lim * dt), mag * jnp.sin(lim * dt)
    den = lre * lre + lim * lim
    fr = ((ar - 1.0) * lre + ai * lim) / den
    fi = (ai * lre - (ar - 1.0) * lim) / den
    are_ref[...] = ar
    aim_ref[...] = ai
    bre, bim = bre_ref[...], bim_ref[...]
    bbre_ref[...] = fr[:, None, :] * bre - fi[:, None, :] * bim
    bbim_ref[...] = fr[:, None, :] * bim + fi[:, None, :] * bre
    s1 = jnp.sum(lq1_ref[...] * lk1_ref[...], axis=-1, keepdims=True)
    s2 = jnp.sum(lq2_ref[...] * lk2_ref[...], axis=-1, keepdims=True)
    lam_ref[...] = jnp.exp(s1) - jnp.exp(s2) + LAM_INIT


def _prep(lam_re, lam_im, log_dt, b_re, b_im, lq1, lk1, lq2, lk2):
    g, p, c = SSM_GROUPS, SSM_STATE, SSM_GROUP
    out_shape = (jax.ShapeDtypeStruct((g, p), F32), jax.ShapeDtypeStruct((g, p), F32),
                 jax.ShapeDtypeStruct((g, c, p), F32), jax.ShapeDtypeStruct((g, c, p), F32),
                 jax.ShapeDtypeStruct((1, 1), F32))
    return pl.pallas_call(_prep_kernel, out_shape=out_shape, name="ssm_prep")(
        lam_re, lam_im, log_dt.reshape(g, 1), jnp.swapaxes(b_re, 1, 2), jnp.swapaxes(b_im, 1, 2),
        lq1.reshape(1, HEAD_DIM), lk1.reshape(1, HEAD_DIM), lq2.reshape(1, HEAD_DIM), lk2.reshape(1, HEAD_DIM))


def _block_diag(blocks):
    g, r, c = blocks.shape
    keep = np.arange(g * r)[:, None] // r == np.arange(g * c)[None, :] // c
    return jnp.where(keep, jnp.tile(blocks.reshape(g * r, c), (1, g)), 0.0)


def _in_proj_kernel(x_ref, g_ref, w_ref, cos_ref, sa_ref, sb_ref,
                    u_ref, q_ref, kf_ref, kb_ref, vf_ref, vb_ref, gs_ref, ga_ref):
    o0 = SSM_WIDTH
    o1, o2, o3, o4 = o0 + ATTN_WIDTH, o0 + 2 * ATTN_WIDTH, o0 + 3 * ATTN_WIDTH, o0 + 3 * ATTN_WIDTH + D_MODEL
    subs = _row_subtiles(x_ref.shape[0])
    sub = subs[0].stop
    heads = [slice(hh * QK_DIM, (hh + 1) * QK_DIM) for hh in range(N_HEADS)]

    def rope(z, rows):
        return (z * cos_ref[rows, :] + pltpu.roll(z, LANES - ROPE_DIM // 2, 1) * sa_ref[rows, :]
                + pltpu.roll(z, ROPE_DIM // 2, 1) * sb_ref[rows, :])

    def seg_u(rows, h):
        u_ref[rows, :] = _dot(h, w_ref[:, 0:o0]).astype(BF16)

    def seg_q(rows, h):
        zq = _dot(h, w_ref[:, o0:o1])
        for sl in heads:
            q_ref[rows, sl] = (rope(zq[:, sl], rows) * Q_SCALE).astype(BF16)

    def seg_k(rows, h):
        zk = _dot(h, w_ref[:, o1:o2])
        for hh, sl in enumerate(heads):
            kr = rope(zk[:, sl], rows)
            kf_ref[pl.ds(rows.start * N_HEADS + hh, sub, stride=N_HEADS), :] = kr
            kb_ref[rows, sl] = kr.astype(BF16)

    def seg_v(rows, h):
        zv = _dot(h, w_ref[:, o2:o3])
        for hh, sl in enumerate(heads):
            vf_ref[pl.ds(rows.start * N_HEADS + hh, sub, stride=N_HEADS), :] = zv[:, sl]
        vb_ref[rows, :] = zv.astype(BF16)

    def seg_gs(rows, h):
        gs_ref[rows, :] = jax.nn.sigmoid(_dot(h, w_ref[:, o3:o4])).astype(BF16)

    def seg_ga(rows, h):
        ga_ref[rows, :] = jax.nn.sigmoid(_dot(h, w_ref[:, o4:IN_WIDTH])).astype(BF16)

    hs = [_rms(x_ref[rows, :], g_ref[...]).astype(BF16) for rows in subs]
    for seg in (seg_u, seg_q, seg_k, seg_v, seg_gs, seg_ga):
        for rows, h in zip(subs, hs):
            seg(rows, h)


def _rope_tables(pos):
    half = ROPE_DIM // 2
    inv = ROPE_THETA ** (-np.arange(half, dtype=np.float64) * 2.0 / ROPE_DIM)
    ang = np.asarray(pos, np.float64)[:, None] * inv[None, :]
    cos, sin = np.cos(ang), np.sin(ang)
    n = ang.shape[0]
    comp_cos = np.concatenate([cos, cos, np.ones((n, HEAD_DIM - ROPE_DIM))], axis=1)
    comp_sa = np.concatenate([-sin, np.zeros((n, HEAD_DIM - half))], axis=1)
    comp_sb = np.concatenate([np.zeros((n, half)), sin, np.zeros((n, HEAD_DIM - ROPE_DIM))], axis=1)
    two = lambda t: jnp.asarray(np.concatenate([t, t], axis=1), F32)
    return two(comp_cos), two(comp_sa), two(comp_sb)


def _in_proj(x2d, pos, nb, g_pre, w_in_bf, tm):
    m = x2d.shape[0]
    t_len = m // nb
    nt = t_len // tm
    cos, sa, sb = _rope_tables(pos)
    row = lambda w: pl.BlockSpec((tm, w), lambda i: (i, 0))
    heads = pl.BlockSpec((tm * N_HEADS, QK_DIM), lambda i: (i, 0))
    tab = pl.BlockSpec((tm, LANES), lambda i: (i % nt, 0))
    out_shape = (jax.ShapeDtypeStruct((t_len, nb * SSM_WIDTH), BF16),
                 jax.ShapeDtypeStruct((m, ATTN_WIDTH), BF16),
                 jax.ShapeDtypeStruct((m * N_HEADS, QK_DIM), F32), jax.ShapeDtypeStruct((m, ATTN_WIDTH), BF16),
                 jax.ShapeDtypeStruct((m * N_HEADS, QK_DIM), F32), jax.ShapeDtypeStruct((m, ATTN_WIDTH), BF16),
                 jax.ShapeDtypeStruct((m, D_MODEL), BF16), jax.ShapeDtypeStruct((m, D_MODEL), BF16))
    out_specs = (pl.BlockSpec((tm, SSM_WIDTH), lambda i: (i % nt, i // nt)),
                 row(ATTN_WIDTH), heads, row(ATTN_WIDTH), heads, row(ATTN_WIDTH),
                 row(D_MODEL), row(D_MODEL))
    return pl.pallas_call(
        _in_proj_kernel, grid=(m // tm,),
        in_specs=[row(D_MODEL), _resident((1, D_MODEL)), _resident((D_MODEL, IN_WIDTH)), tab, tab, tab],
        out_specs=out_specs, out_shape=out_shape,
        compiler_params=_params(("parallel",)), name="in_proj")(x2d, g_pre, w_in_bf, cos, sa, sb)


def _ssm_drive(u, bd_ref, store):
    n_tiles = 2 * SSM_LANES // 256
    for j in range(n_tiles):
        ks = LANES * ((j % (n_tiles // 2)) // 2)
        lanes = slice(256 * j, 256 * (j + 1))
        store(lanes, _dot(u[:, ks:ks + LANES], bd_ref[ks:ks + LANES, lanes]))


def _ssm_readout(load, u, cre_ref, cim_ref, dsk_ref):
    half = SSM_LANES // 2
    ys = []
    for j in range(2):
        hre = load(slice(half * j, half * (j + 1))).astype(BF16)
        him = load(slice(SSM_LANES + half * j, SSM_LANES + half * (j + 1))).astype(BF16)
        cols = slice(256 * j, 256 * (j + 1))
        ys.append(_dot(hre, cre_ref[half * j:half * (j + 1), cols]) + _dot(him, cim_ref[half * j:half * (j + 1), cols]))
    y = jnp.concatenate(ys, axis=1) + dsk_ref[...] * u.astype(F32)
    return jax.nn.gelu(y)


def _glu(yg, wglu_ref):
    return _dot(yg, wglu_ref[:, 0:D_MODEL]) * jax.nn.sigmoid(_dot(yg, wglu_ref[:, D_MODEL:2 * D_MODEL]))


def _ssm_prompt_kernel(u_ref, are_ref, aim_ref, bd_ref, cre_ref, cim_ref, dsk_ref, wglu_ref,
                       ys_ref, hre_ref, him_ref, h_s, il_s, st_re, st_im, *, nb, lc):
    @pl.when(pl.program_id(0) == 0)
    def _():
        st_re[...] = jnp.zeros_like(st_re)
        st_im[...] = jnp.zeros_like(st_im)

    rows = lc * nb
    n_tr = rows // SUBLANES
    n_slabs = SSM_WIDTH // LANES
    slab = lambda j: slice(j * LANES, (j + 1) * LANES)
    seq_rows = lambda b: pl.ds(b, lc, stride=nb)
    for b in range(nb):
        ub = u_ref[:, b * SSM_WIDTH:(b + 1) * SSM_WIDTH].astype(F32)
        for j in range(n_slabs):
            il_s[j, seq_rows(b), :] = ub[:, slab(j)]
    u = jnp.concatenate([il_s[j] for j in range(n_slabs)], axis=1).astype(BF16)

    def store(lanes, val):
        h_s[:, :, lanes] = val.reshape(n_tr, SUBLANES, val.shape[-1])

    _ssm_drive(u, bd_ref, store)

    def step(s, carry):
        hr, hi = carry
        ar, ai = are_ref[...], aim_ref[...]
        for k in range(SUBLANES // nb):
            sub = slice(k * nb, (k + 1) * nb)
            nhr = ar * hr - ai * hi + h_s[s, sub, 0:SSM_LANES]
            nhi = ar * hi + ai * hr + h_s[s, sub, SSM_LANES:2 * SSM_LANES]
            h_s[s, sub, 0:SSM_LANES] = nhr
            h_s[s, sub, SSM_LANES:2 * SSM_LANES] = nhi
            hr, hi = nhr, nhi
        return hr, hi

    hr, hi = lax.fori_loop(0, n_tr, step, (st_re[...], st_im[...]))
    st_re[...] = hr
    st_im[...] = hi
    hre_ref[...] = hr
    him_ref[...] = hi

    load = lambda lanes: h_s[:, :, lanes].reshape(rows, lanes.stop - lanes.start)
    yg = _ssm_readout(load, u, cre_ref, cim_ref, dsk_ref)
    for j in range(n_slabs):
        il_s[j] = yg[:, slab(j)]
    per_seq = [jnp.concatenate([il_s[j, seq_rows(b), :] for j in range(n_slabs)], axis=1) for b in range(nb)]
    res = _glu(jnp.concatenate(per_seq, axis=0).astype(BF16), wglu_ref)
    for b in range(nb):
        ys_ref[b] = res[b * lc:(b + 1) * lc].astype(BF16)


def _ssm_prompt(u_il, nb, t_len, are, aim, bd, cre, cim, dsk, wglu, lc):
    rows = lc * nb
    kern = functools.partial(_ssm_prompt_kernel, nb=nb, lc=lc)
    out_shape = (jax.ShapeDtypeStruct((nb, t_len, D_MODEL), BF16),
                 jax.ShapeDtypeStruct((nb, SSM_LANES), F32), jax.ShapeDtypeStruct((nb, SSM_LANES), F32))
    return pl.pallas_call(
        kern, grid=(t_len // lc,),
        in_specs=[pl.BlockSpec((lc, nb * SSM_WIDTH), lambda c: (c, 0)),
                  _resident((nb, SSM_LANES)), _resident((nb, SSM_LANES)),
                  _resident((SSM_WIDTH, 2 * SSM_LANES)), _resident((SSM_LANES, SSM_WIDTH)),
                  _resident((SSM_LANES, SSM_WIDTH)), _resident((1, SSM_WIDTH)),
                  _resident((SSM_WIDTH, 2 * D_MODEL))],
        out_specs=(pl.BlockSpec((nb, lc, D_MODEL), lambda c: (0, c, 0)),
                   pl.BlockSpec((nb, SSM_LANES), lambda c: (0, 0)), pl.BlockSpec((nb, SSM_LANES), lambda c: (0, 0))),
        out_shape=out_shape,
        scratch_shapes=[pltpu.VMEM((rows // SUBLANES, SUBLANES, 2 * SSM_LANES), F32),
                        pltpu.VMEM((SSM_WIDTH // LANES, rows, LANES), F32),
                        pltpu.VMEM((nb, SSM_LANES), F32), pltpu.VMEM((nb, SSM_LANES), F32)],
        compiler_params=_params(("arbitrary",)), name="ssm_prompt")(
            u_il, jnp.tile(are, (nb, 1)), jnp.tile(aim, (nb, 1)), bd, cre, cim, dsk, wglu)


def _ssm_step_kernel(u_ref, h0re_ref, h0im_ref, are_ref, aim_ref, bd_ref, cre_ref, cim_ref, dsk_ref, wglu_ref,
                     ys_ref, hre_ref, him_ref, h_s):
    u = u_ref[...]

    def store(lanes, val):
        h_s[:, lanes] = val

    _ssm_drive(u, bd_ref, store)
    ar, ai = are_ref[...], aim_ref[...]
    hr, hi = h0re_ref[...], h0im_ref[...]
    nhr = ar * hr - ai * hi + h_s[:, 0:SSM_LANES]
    nhi = ar * hi + ai * hr + h_s[:, SSM_LANES:2 * SSM_LANES]
    hre_ref[...] = nhr
    him_ref[...] = nhi
    h_s[:, 0:SSM_LANES] = nhr
    h_s[:, SSM_LANES:2 * SSM_LANES] = nhi
    yg = _ssm_readout(lambda lanes: h_s[:, lanes], u, cre_ref, cim_ref, dsk_ref)
    ys_ref[...] = _glu(yg.astype(BF16), wglu_ref).astype(BF16)


def _ssm_step(u, h0re, h0im, are, aim, bd, cre, cim, dsk, wglu):
    n = u.shape[0]
    out_shape = (jax.ShapeDtypeStruct((n, D_MODEL), BF16),
                 jax.ShapeDtypeStruct((n, SSM_LANES), F32), jax.ShapeDtypeStruct((n, SSM_LANES), F32))
    return pl.pallas_call(
        _ssm_step_kernel, out_shape=out_shape,
        scratch_shapes=[pltpu.VMEM((n, 2 * SSM_LANES), F32)],
        compiler_params=pltpu.CompilerParams(vmem_limit_bytes=VMEM_LIMIT), name="ssm_step")(
            u, h0re, h0im, are, aim, bd, cre, cim, dsk, wglu)


def _subln(o1, l1, o2, l2, lam, gain):
    o = o1 / l1 - lam * (o2 / l2)
    return _rms(o, gain) * (1.0 - LAM_INIT)


def _prompt_tile(qi, lam_ref, q_ref, k_ref, v_ref, gain_ref, o_ref, sa_s, sb_s, m_s, acc_s, tq, tk):
    lane = lax.broadcasted_iota(jnp.int32, (tk, QK_DIM), 1)
    parts = []
    for half in range(2):
        q = q_ref[half * tk:(half + 1) * tk, :]
        zero = jnp.zeros_like(q)
        parts += [jnp.where(lane < HEAD_DIM, q, zero), jnp.where(lane >= HEAD_DIM, q, zero)]
    qz = jnp.concatenate(parts, axis=0)
    m_s[...] = jnp.full_like(m_s, -jnp.inf)
    acc_s[...] = jnp.zeros_like(acc_s)
    ones = jnp.ones((tk, LANES), BF16)
    wide = lambda a: jnp.concatenate([a] * (tk // LANES), axis=1)

    def scores(c, s_ref, rows=slice(0, 2 * tq)):
        r0 = pl.multiple_of(c * tk, tk)
        s_ref[rows, :] = _dot_nt(qz[rows], k_ref[pl.ds(r0, tk), :])

    def softmax_pv(c, s_ref, rows=slice(0, 2 * tq), diag_rows=0):
        s = s_ref[rows, :]
        if diag_rows:
            row = lax.broadcasted_iota(jnp.int32, (diag_rows, tk), 0) % tk
            col = lax.broadcasted_iota(jnp.int32, (diag_rows, tk), 1)
            masked = jnp.where(col <= row, s[0:diag_rows], -jnp.inf)
            s = masked if diag_rows == s.shape[0] else jnp.concatenate([masked, s[diag_rows:]], axis=0)
        m_prev = m_s[rows, :]
        m_new = jnp.maximum(m_prev, jnp.max(s, axis=1, keepdims=True))
        alpha = jnp.exp2(m_prev - m_new)
        p = jnp.exp2(s - wide(m_new)).astype(BF16)
        r0 = pl.multiple_of(c * tk, tk)
        v1 = jnp.concatenate([v_ref[pl.ds(r0, tk), :], ones], axis=1)
        acc_s[rows, :] = jnp.concatenate([alpha, alpha], axis=1) * acc_s[rows, :] + _dot(p, v1)
        m_s[rows, :] = m_new

    scores(0, sa_s)

    def body(j, carry):
        c0 = 2 * j
        scores(c0 + 1, sb_s)
        softmax_pv(c0, sa_s)
        scores(c0 + 2, sa_s)
        softmax_pv(c0 + 1, sb_s)
        return carry

    lax.fori_loop(0, qi, body, 0)
    c0 = 2 * qi
    late = slice(tq, 2 * tq)
    scores(c0 + 1, sb_s, late)
    softmax_pv(c0, sa_s, diag_rows=tq)
    softmax_pv(c0 + 1, sb_s, late, diag_rows=tq)
    acc = acc_s[...]
    lam, gain = lam_ref[0, 0], gain_ref[...]
    for half in range(2):
        a1 = acc[2 * half * tk:(2 * half + 1) * tk]
        a2 = acc[(2 * half + 1) * tk:(2 * half + 2) * tk]
        o_ref[half * tk:(half + 1) * tk, :] = _subln(a1[:, 0:QK_DIM], a1[:, QK_DIM:2 * QK_DIM], a2[:, 0:QK_DIM],
                                                     a2[:, QK_DIM:2 * QK_DIM], lam, gain).astype(BF16)


def _decode_query(q_ref):
    q8 = q_ref[0].astype(F32)
    lane = lax.broadcasted_iota(jnp.int32, (N_HEADS, QK_DIM), 1)
    qz32 = jnp.concatenate([jnp.where(lane < HEAD_DIM, q8, 0.0), jnp.where(lane >= HEAD_DIM, q8, 0.0)], axis=0)
    return qz32, qz32.astype(BF16)


def _decode_groups(q_ref, k_groups, v_groups):
    n_maps = 2 * N_HEADS
    rows_pp = PAGE_SIZE * N_HEADS
    _, qz = _decode_query(q_ref)
    scores = [jnp.concatenate([_dot_nt(qz, r[...].astype(BF16)) for r in ks], axis=1) for ks in k_groups]
    stats, probs = [], []
    for s in scores:
        col = lax.broadcasted_iota(jnp.int32, s.shape, 1)
        row = lax.broadcasted_iota(jnp.int32, s.shape, 0)
        s = jnp.where(col % N_HEADS == row % N_HEADS, s, -jnp.inf)
        m = jnp.max(s, axis=1, keepdims=True)
        p = jnp.exp2(s - m)
        stats.append((m, jnp.sum(p, axis=1, keepdims=True)))
        probs.append(p.astype(BF16))
    out = []
    for (m, l), pb, vs in zip(stats, probs, v_groups):
        pv = _dot(pb[:, 0:rows_pp], vs[0][...].astype(BF16))
        for i in range(1, len(vs)):
            pv = pv + _dot(pb[:, i * rows_pp:(i + 1) * rows_pp], vs[i][...].astype(BF16))
        out.append((m, l, pv))
    return out


def _decode_finish(groups, lam_ref, gain_ref, q_ref, kn_ref, vn_ref, o_ref):
    n_maps = 2 * N_HEADS
    qz32, _ = _decode_query(q_ref)
    two = lambda a: jnp.concatenate([a, a], axis=0)
    s_n = jnp.sum(qz32 * two(kn_ref[0].astype(F32)), axis=1, keepdims=True)
    m = s_n
    for mg, _, _ in groups:
        m = jnp.maximum(m, mg)
    l = jnp.exp2(s_n - m)
    acc = l * two(vn_ref[0].astype(F32))
    for mg, lg, pvg in groups:
        w = jnp.exp2(mg - m)
        l = l + w * lg
        acc = acc + w * pvg
    o_ref[0] = _subln(acc[0:N_HEADS], l[0:N_HEADS], acc[N_HEADS:n_maps], l[N_HEADS:n_maps], lam_ref[0, 0],
                      gain_ref[...]).astype(BF16)


def _attn_kernel(pt_ref, lam_ref, q_ref, k_ref, v_ref, gain_ref, qs_ref, *refs, tq, tk, n_pages):
    del pt_ref
    k_refs, v_refs = refs[0:n_pages], refs[n_pages:2 * n_pages]
    kn_ref, vn_ref, o_ref, os_ref, sa_s, sb_s, m_s, acc_s = refs[2 * n_pages:]
    _prompt_tile(pl.program_id(2), lam_ref, q_ref, k_ref, v_ref, gain_ref, o_ref, sa_s, sb_s, m_s, acc_s, tq, tk)
    per = n_pages // DECODE_GROUPS
    split = lambda page_refs: [page_refs[g * per:(g + 1) * per] for g in range(DECODE_GROUPS)]
    groups = _decode_groups(qs_ref, split(k_refs), split(v_refs))
    _decode_finish(groups, lam_ref, gain_ref, qs_ref, kn_ref, vn_ref, os_ref)


def _attention(page_table, lam, q, k, v, gain, q_s, cache_k, cache_v, k_new, v_new, nb, t_len, tq):
    tk = tq // 2
    n, n_pages = page_table.shape
    nq = t_len // tq
    assert nb * N_HEADS * nq == n, "one decode sample per prompt grid step"
    kern = functools.partial(_attn_kernel, tq=tq, tk=tk, n_pages=n_pages)
    sample = lambda b, h, i: (b * N_HEADS + h) * nq + i
    qspec = pl.BlockSpec((None, tq, QK_DIM), lambda b, h, i, pt: (b, i, h))
    kvspec = pl.BlockSpec((None, t_len, QK_DIM), lambda b, h, i, pt: (b, 0, h))
    row = pl.BlockSpec((1, N_HEADS, QK_DIM), lambda b, h, i, pt: (sample(b, h, i), 0, 0))

    def page(j):
        return pl.BlockSpec((None, PAGE_SIZE * N_HEADS, QK_DIM), lambda b, h, i, pt: (pt[sample(b, h, i), j], 0, 0))

    pages = [page(j) for j in range(n_pages)]
    grid_spec = pltpu.PrefetchScalarGridSpec(
        num_scalar_prefetch=1, grid=(nb, N_HEADS, nq),
        in_specs=[pl.BlockSpec(memory_space=pltpu.SMEM), qspec, kvspec, kvspec,
                  pl.BlockSpec((1, QK_DIM), lambda b, h, i, pt: (0, 0)), row] + pages + pages + [row, row],
        out_specs=(qspec, row),
        scratch_shapes=[pltpu.VMEM((2 * tq, tk), F32), pltpu.VMEM((2 * tq, tk), F32),
                        pltpu.VMEM((2 * tq, LANES), F32), pltpu.VMEM((2 * tq, 2 * QK_DIM), F32)])
    r3 = lambda a: a.reshape(n, N_HEADS, QK_DIM)
    out_shape = (jax.ShapeDtypeStruct((nb, t_len, ATTN_WIDTH), BF16), jax.ShapeDtypeStruct((n, N_HEADS, QK_DIM), BF16))
    return pl.pallas_call(
        kern, grid_spec=grid_spec, out_shape=out_shape,
        compiler_params=_params(("arbitrary", "arbitrary", "arbitrary"), ATTN_VMEM_LIMIT), name="attention")(
            page_table, lam, q, k, v, gain, r3(q_s), *([cache_k] * n_pages), *([cache_v] * n_pages),
            r3(k_new), r3(v_new))


FF_CHUNKS = ((0, 1024), (1024, 2048), (2048, D_FF))


def _out_ffn_kernel(x_ref, ys_ref, o_ref, gs_ref, ga_ref, wo_ref, npm_ref, npf_ref, wg_ref, wu_ref, wd_ref, nof_ref,
                    y_ref):
    subs = _row_subtiles(x_ref.shape[0])
    f32 = lambda ref, rows: ref[rows, :].astype(F32)
    mixed = [(f32(gs_ref, r) * f32(ys_ref, r) + f32(ga_ref, r) * f32(o_ref, r)).astype(BF16) for r in subs]
    proj = [_dot(m, wo_ref[...]) for m in mixed]
    x1 = [x_ref[r, :] + _rms(p, npm_ref[...]) for r, p in zip(subs, proj)]
    hf = [_rms(x, npf_ref[...]).astype(BF16) for x in x1]
    f = [None] * len(subs)
    for c0, c1 in FF_CHUNKS:
        for i, h in enumerate(hf):
            act = (jax.nn.silu(_dot(h, wg_ref[:, c0:c1])) * _dot(h, wu_ref[:, c0:c1])).astype(BF16)
            part = _dot(act, wd_ref[c0:c1, :])
            f[i] = part if f[i] is None else f[i] + part
    for r, x, fi in zip(subs, x1, f):
        y_ref[r, :] = x + _rms(fi, nof_ref[...])


def _out_ffn(x2d, ys, o, gs, ga, wo, npm, npf, wg, wu, wd, nof, tm):
    m = x2d.shape[0]
    row = pl.BlockSpec((tm, D_MODEL), lambda i: (i, 0))
    vec = _resident((1, D_MODEL))
    return pl.pallas_call(
        _out_ffn_kernel, grid=(m // tm,),
        in_specs=[row, row, row, row, row, _resident((D_MODEL, D_MODEL)), vec, vec,
                  _resident((D_MODEL, D_FF)), _resident((D_MODEL, D_FF)), _resident((D_FF, D_MODEL)), vec],
        out_specs=row, out_shape=jax.ShapeDtypeStruct((m, D_MODEL), F32),
        compiler_params=_params(("parallel",)), name="out_ffn")(x2d, ys, o, gs, ga, wo, npm, npf, wg, wu, wd, nof)


def kernel(x_prompt, x_sample, cache_k, cache_v, state_ssm_re, state_ssm_im, page_table, norm_pre_mix, w_in,
           ssm_lambda_re, ssm_lambda_im, ssm_log_dt, ssm_b_re, ssm_b_im, ssm_c_re, ssm_c_im, ssm_d, w_glu_a,
           w_glu_b, lambda_q1, lambda_k1, lambda_q2, lambda_k2, subln_gain, w_o, norm_post_mix, norm_pre_ffn,
           w_gate, w_up, w_down, norm_post_ffn):
    assert w_in.shape[0] == 1, "single-layer stack"
    bp, tp, _ = x_prompt.shape
    bs, ts, _ = x_sample.shape
    assert ts == 1

    are, aim, bbre, bbim, lam = _prep(ssm_lambda_re[0], ssm_lambda_im[0], ssm_log_dt[0], ssm_b_re[0], ssm_b_im[0],
                                      lambda_q1[0], lambda_k1[0], lambda_q2[0], lambda_k2[0])
    are, aim = are.reshape(1, SSM_LANES), aim.reshape(1, SSM_LANES)
    bd = jnp.concatenate([_block_diag(bbre), _block_diag(bbim)], axis=1).astype(BF16)
    cre = _block_diag(jnp.swapaxes(ssm_c_re[0], 1, 2)).astype(BF16)
    cim = _block_diag(jnp.swapaxes(-ssm_c_im[0], 1, 2)).astype(BF16)
    dsk = ssm_d[0].reshape(1, SSM_WIDTH)
    wglu = jnp.concatenate([w_glu_a[0], w_glu_b[0]], axis=1).astype(BF16)
    vec = lambda a: a[0].reshape(1, -1)
    w_in_bf, wo_bf = w_in[0].astype(BF16), w_o[0].astype(BF16)
    wg_bf, wu_bf, wd_bf = w_gate[0].astype(BF16), w_up[0].astype(BF16), w_down[0].astype(BF16)
    gain = vec(subln_gain)
    ssm_w = (are, aim, bd, cre, cim, dsk, wglu)
    ffn_w = (wo_bf, vec(norm_post_mix), vec(norm_pre_ffn), wg_bf, wu_bf, wd_bf, vec(norm_post_ffn))

    xp = x_prompt.reshape(bp * tp, D_MODEL)
    u_il, q, kf, kb, vf, vb, gs, ga = _in_proj(xp, np.arange(tp), bp, vec(norm_pre_mix), w_in_bf, tm=512)
    ys, hpr, hpi = _ssm_prompt(u_il, bp, tp, *ssm_w, lc=128)

    xs = x_sample.reshape(bs, D_MODEL)
    pos_s = np.full((bs,), PAST_LEN)
    u_s, q_s, kf_s, kb_s, vf_s, vb_s, gs_s, ga_s = _in_proj(xs, pos_s, 1, vec(norm_pre_mix), w_in_bf, tm=bs)
    ys_s, hsr, hsi = _ssm_step(u_s, state_ssm_re[0].reshape(bs, SSM_LANES), state_ssm_im[0].reshape(bs, SSM_LANES),
                               *ssm_w)

    n_phys = cache_k.shape[1]
    pages = lambda c: c[0].reshape(n_phys, PAGE_SIZE * N_HEADS, QK_DIM)
    r3 = lambda a: a.reshape(bp, tp, ATTN_WIDTH)
    o, o_s = _attention(page_table, lam, r3(q), r3(kb), r3(vb), gain, q_s, pages(cache_k), pages(cache_v),
                        kb_s, vb_s, bp, tp, tq=1024)

    yp = _out_ffn(xp, ys.reshape(bp * tp, D_MODEL), o.reshape(bp * tp, ATTN_WIDTH), gs, ga, *ffn_w, tm=512)
    ysamp = _out_ffn(xs, ys_s, o_s.reshape(bs, ATTN_WIDTH), gs_s, ga_s, *ffn_w, tm=bs)

    st = lambda a, n: a.reshape(1, n, SSM_GROUPS, SSM_STATE)
    kv = lambda a, n, t: a.reshape(1, n, t, N_HEADS, QK_DIM)
    return (yp.reshape(bp, tp, D_MODEL), ysamp.reshape(bs, ts, D_MODEL),
            kv(kf, bp, tp), kv(vf, bp, tp), st(hpr, bp), st(hpi, bp),
            kv(kf_s, bs, ts), kv(vf_s, bs, ts), st(hsr, bs), st(hsi, bs))
```

```python
import functools
import math

import jax
import jax.numpy as jnp
import numpy as np
from jax import lax
from jax.experimental import pallas as pl
from jax.experimental.pallas import tpu as pltpu

F32 = jnp.float32
BF16 = jnp.bfloat16

D_MODEL = 1024
PAST_LEN = 2048
PAGE_SIZE = 128
SSM_WIDTH = 512
SSM_GROUP = 16
SSM_GROUPS = 32
SSM_STATE = 64
SSM_LANES = SSM_GROUPS * SSM_STATE
HEAD_DIM = 64
N_HEADS = 8
QK_DIM = 128
ATTN_WIDTH = 1024
ROPE_DIM = 16
ROPE_THETA = 500000.0
D_FF = 2816
IN_WIDTH = SSM_WIDTH + 3 * ATTN_WIDTH + 2 * D_MODEL
EPS = 1e-6
LAM_INIT = 0.8 - 0.6 * math.exp(-0.3 * 0)
Q_SCALE = HEAD_DIM ** -0.5 * math.log2(math.e)

LANES = 128
SUBLANES = 8
VMEM_LIMIT = 56 * 1024 * 1024
ATTN_VMEM_LIMIT = 62 * 1024 * 1024
DECODE_GROUPS = 2


ROW_SUBTILES = 2
MIN_SUBTILE_ROWS = 256


def _params(sem, vmem=VMEM_LIMIT):
    return pltpu.CompilerParams(dimension_semantics=sem, vmem_limit_bytes=vmem)


def _row_subtiles(tm):
    sub = max(tm // ROW_SUBTILES, min(tm, MIN_SUBTILE_ROWS))
    return [slice(r0, r0 + sub) for r0 in range(0, tm, sub)]


def _resident(shape):
    return pl.BlockSpec(shape, lambda *_: (0,) * len(shape), pipeline_mode=pl.Buffered(1))


def _rms(x, g):
    return x * lax.rsqrt(jnp.mean(x * x, axis=-1, keepdims=True) + EPS) * g


def _dot(a, b):
    return jnp.dot(a, b, preferred_element_type=F32)


def _dot_nt(a, b):
    return lax.dot_general(a, b, (((1,), (1,)), ((), ())), preferred_element_type=F32)


def _prep_kernel(lre_ref, lim_ref, ldt_ref, bre_ref, bim_ref, lq1_ref, lk1_ref, lq2_ref, lk2_ref,
                 are_ref, aim_ref, bbre_ref, bbim_ref, lam_ref):
    lre, lim = lre_ref[...], lim_ref[...]
    dt = jnp.exp(ldt_ref[...])
    mag = jnp.exp(lre * dt)
    ar, ai = mag * jnp.cos(lim * dt), mag * jnp.sin(lim * dt)
    den = lre * lre + lim * lim
    fr = ((ar - 1.0) * lre + ai * lim) / den
    fi = (ai * lre - (ar - 1.0) * lim) / den
    are_ref[...] = ar
    aim_ref[...] = ai
    bre, bim = bre_ref[...], bim_ref[...]
    bbre_ref[...] = fr[:, None, :] * bre - fi[:, None, :] * bim
    bbim_ref[...] = fr[:, None, :] * bim + fi[:, None, :] * bre
    s1 = jnp.sum(lq1_ref[...] * lk1_ref[...], axis=-1, keepdims=True)
    s2 = jnp.sum(lq2_ref[...] * lk2_ref[...], axis=-1, keepdims=True)
    lam_ref[...] = jnp.exp(s1) - jnp.exp(s2) + LAM_INIT


def _prep(lam_re, lam_im, log_dt, b_re, b_im, lq1, lk1, lq2, lk2):
    g, p, c = SSM_GROUPS, SSM_STATE, SSM_GROUP
    out_shape = (jax.ShapeDtypeStruct((g, p), F32), jax.ShapeDtypeStruct((g, p), F32),
                 jax.ShapeDtypeStruct((g, c, p), F32), jax.ShapeDtypeStruct((g, c, p), F32),
                 jax.ShapeDtypeStruct((1, 1), F32))
    return pl.pallas_call(_prep_kernel, out_shape=out_shape, name="ssm_prep")(
        lam_re, lam_im, log_dt.reshape(g, 1), jnp.swapaxes(b_re, 1, 2), jnp.swapaxes(b_im, 1, 2),
        lq1.reshape(1, HEAD_DIM), lk1.reshape(1, HEAD_DIM), lq2.reshape(1, HEAD_DIM), lk2.reshape(1, HEAD_DIM))


def _block_diag(blocks):
    g, r, c = blocks.shape
    keep = np.arange(g * r)[:, None] // r == np.arange(g * c)[None, :] // c
    return jnp.where(keep, jnp.tile(blocks.reshape(g * r, c), (1, g)), 0.0)


def _in_proj_kernel(x_ref, g_ref, w_ref, cos_ref, sa_ref, sb_ref,
                    u_ref, q_ref, kf_ref, kb_ref, vf_ref, vb_ref, gs_ref, ga_ref):
    o0 = SSM_WIDTH
    o1, o2, o3, o4 = o0 + ATTN_WIDTH, o0 + 2 * ATTN_WIDTH, o0 + 3 * ATTN_WIDTH, o0 + 3 * ATTN_WIDTH + D_MODEL
    subs = _row_subtiles(x_ref.shape[0])
    sub = subs[0].stop
    heads = [slice(hh * QK_DIM, (hh + 1) * QK_DIM) for hh in range(N_HEADS)]

    def rope(z, rows):
        return (z * cos_ref[rows, :] + pltpu.roll(z, LANES - ROPE_DIM // 2, 1) * sa_ref[rows, :]
                + pltpu.roll(z, ROPE_DIM // 2, 1) * sb_ref[rows, :])

    def seg_u(rows, h):
        u_ref[rows, :] = _dot(h, w_ref[:, 0:o0]).astype(BF16)

    def seg_q(rows, h):
        zq = _dot(h, w_ref[:, o0:o1])
        for sl in heads:
            q_ref[rows, sl] = (rope(zq[:, sl], rows) * Q_SCALE).astype(BF16)

    def seg_k(rows, h):
        zk = _dot(h, w_ref[:, o1:o2])
        for hh, sl in enumerate(heads):
            kr = rope(zk[:, sl], rows)
            kf_ref[pl.ds(rows.start * N_HEADS + hh, sub, stride=N_HEADS), :] = kr
            kb_ref[rows, sl] = kr.astype(BF16)

    def seg_v(rows, h):
        zv = _dot(h, w_ref[:, o2:o3])
        for hh, sl in enumerate(heads):
            vf_ref[pl.ds(rows.start * N_HEADS + hh, sub, stride=N_HEADS), :] = zv[:, sl]
        vb_ref[rows, :] = zv.astype(BF16)

    def seg_gs(rows, h):
        gs_ref[rows, :] = jax.nn.sigmoid(_dot(h, w_ref[:, o3:o4])).astype(BF16)

    def seg_ga(rows, h):
        ga_ref[rows, :] = jax.nn.sigmoid(_dot(h, w_ref[:, o4:IN_WIDTH])).astype(BF16)

    hs = [_rms(x_ref[rows, :], g_ref[...]).astype(BF16) for rows in subs]
    for seg in (seg_u, seg_q, seg_k, seg_v, seg_gs, seg_ga):
        for rows, h in zip(subs, hs):
            seg(rows, h)


def _rope_tables(pos):
    half = ROPE_DIM // 2
    inv = ROPE_THETA ** (-np.arange(half, dtype=np.float64) * 2.0 / ROPE_DIM)
    ang = np.asarray(pos, np.float64)[:, None] * inv[None, :]
    cos, sin = np.cos(ang), np.sin(ang)
    n = ang.shape[0]
    comp_cos = np.concatenate([cos, cos, np.ones((n, HEAD_DIM - ROPE_DIM))], axis=1)
    comp_sa = np.concatenate([-sin, np.zeros((n, HEAD_DIM - half))], axis=1)
    comp_sb = np.concatenate([np.zeros((n, half)), sin, np.zeros((n, HEAD_DIM - ROPE_DIM))], axis=1)
    two = lambda t: jnp.asarray(np.concatenate([t, t], axis=1), F32)
    return two(comp_cos), two(comp_sa), two(comp_sb)


def _in_proj(x2d, pos, nb, g_pre, w_in_bf, tm):
    m = x2d.shape[0]
    t_len = m // nb
    nt = t_len // tm
    cos, sa, sb = _rope_tables(pos)
    row = lambda w: pl.BlockSpec((tm, w), lambda i: (i, 0))
    heads = pl.BlockSpec((tm * N_HEADS, QK_DIM), lambda i: (i, 0))
    tab = pl.BlockSpec((tm, LANES), lambda i: (i % nt, 0))
    out_shape = (jax.ShapeDtypeStruct((t_len, nb * SSM_WIDTH), BF16),
                 jax.ShapeDtypeStruct((m, ATTN_WIDTH), BF16),
                 jax.ShapeDtypeStruct((m * N_HEADS, QK_DIM), F32), jax.ShapeDtypeStruct((m, ATTN_WIDTH), BF16),
                 jax.ShapeDtypeStruct((m * N_HEADS, QK_DIM), F32), jax.ShapeDtypeStruct((m, ATTN_WIDTH), BF16),
                 jax.ShapeDtypeStruct((m, D_MODEL), BF16), jax.ShapeDtypeStruct((m, D_MODEL), BF16))
    out_specs = (pl.BlockSpec((tm, SSM_WIDTH), lambda i: (i % nt, i // nt)),
                 row(ATTN_WIDTH), heads, row(ATTN_WIDTH), heads, row(ATTN_WIDTH),
                 row(D_MODEL), row(D_MODEL))
    return pl.pallas_call(
        _in_proj_kernel, grid=(m // tm,),
        in_specs=[row(D_MODEL), _resident((1, D_MODEL)), _resident((D_MODEL, IN_WIDTH)), tab, tab, tab],
        out_specs=out_specs, out_shape=out_shape,
        compiler_params=_params(("parallel",)), name="in_proj")(x2d, g_pre, w_in_bf, cos, sa, sb)


def _ssm_drive(u, bd_ref, store):
    n_tiles = 2 * SSM_LANES // 256
    for j in range(n_tiles):
        ks = LANES * ((j % (n_tiles // 2)) // 2)
        lanes = slice(256 * j, 256 * (j + 1))
        store(lanes, _dot(u[:, ks:ks + LANES], bd_ref[ks:ks + LANES, lanes]))


def _ssm_readout(load, u, cre_ref, cim_ref, dsk_ref):
    half = SSM_LANES // 2
    ys = []
    for j in range(2):
        hre = load(slice(half * j, half * (j + 1))).astype(BF16)
        him = load(slice(SSM_LANES + half * j, SSM_LANES + half * (j + 1))).astype(BF16)
        cols = slice(256 * j, 256 * (j + 1))
        ys.append(_dot(hre, cre_ref[half * j:half * (j + 1), cols]) + _dot(him, cim_ref[half * j:half * (j + 1), cols]))
    y = jnp.concatenate(ys, axis=1) + dsk_ref[...] * u.astype(F32)
    return jax.nn.gelu(y)


def _glu(yg, wglu_ref):
    return _dot(yg, wglu_ref[:, 0:D_MODEL]) * jax.nn.sigmoid(_dot(yg, wglu_ref[:, D_MODEL:2 * D_MODEL]))


def _ssm_prompt_kernel(u_ref, are_ref, aim_ref, bd_ref, cre_ref, cim_ref, dsk_ref, wglu_ref,
                       ys_ref, hre_ref, him_ref, h_s, il_s, st_re, st_im, *, nb, lc):
    @pl.when(pl.program_id(0) == 0)
    def _():
        st_re[...] = jnp.zeros_like(st_re)
        st_im[...] = jnp.zeros_like(st_im)

    rows = lc * nb
    n_tr = rows // SUBLANES
    n_slabs = SSM_WIDTH // LANES
    slab = lambda j: slice(j * LANES, (j + 1) * LANES)
    seq_rows = lambda b: pl.ds(b, lc, stride=nb)
    for b in range(nb):
        ub = u_ref[:, b * SSM_WIDTH:(b + 1) * SSM_WIDTH].astype(F32)
        for j in range(n_slabs):
            il_s[j, seq_rows(b), :] = ub[:, slab(j)]
    u = jnp.concatenate([il_s[j] for j in range(n_slabs)], axis=1).astype(BF16)

    def store(lanes, val):
        h_s[:, :, lanes] = val.reshape(n_tr, SUBLANES, val.shape[-1])

    _ssm_drive(u, bd_ref, store)

    ar, ai = are_ref[...], aim_ref[...]
    hr, hi = st_re[...], st_im[...]
    for s in range(n_tr):
        for k in range(SUBLANES // nb):
            sub = slice(k * nb, (k + 1) * nb)
            nhr = ar * hr - ai * hi + h_s[s, sub, 0:SSM_LANES]
            nhi = ar * hi + ai * hr + h_s[s, sub, SSM_LANES:2 * SSM_LANES]
            h_s[s, sub, 0:SSM_LANES] = nhr
            h_s[s, sub, SSM_LANES:2 * SSM_LANES] = nhi
            hr, hi = nhr, nhi
    st_re[...] = hr
    st_im[...] = hi
    hre_ref[...] = hr
    him_ref[...] = hi

    load = lambda lanes: h_s[:, :, lanes].reshape(rows, lanes.stop - lanes.start)
    yg = _ssm_readout(load, u, cre_ref, cim_ref, dsk_ref)
    for j in range(n_slabs):
        il_s[j] = yg[:, slab(j)]
    per_seq = [jnp.concatenate([il_s[j, seq_rows(b), :] for j in range(n_slabs)], axis=1) for b in range(nb)]
    res = _glu(jnp.concatenate(per_seq, axis=0).astype(BF16), wglu_ref)
    for b in range(nb):
        ys_ref[b] = res[b * lc:(b + 1) * lc].astype(BF16)


def _ssm_prompt(u_il, nb, t_len, are, aim, bd, cre, cim, dsk, wglu, lc):
    rows = lc * nb
    kern = functools.partial(_ssm_prompt_kernel, nb=nb, lc=lc)
    out_shape = (jax.ShapeDtypeStruct((nb, t_len, D_MODEL), BF16),
                 jax.ShapeDtypeStruct((nb, SSM_LANES), F32), jax.ShapeDtypeStruct((nb, SSM_LANES), F32))
    return pl.pallas_call(
        kern, grid=(t_len // lc,),
        in_specs=[pl.BlockSpec((lc, nb * SSM_WIDTH), lambda c: (c, 0)),
                  _resident((nb, SSM_LANES)), _resident((nb, SSM_LANES)),
                  _resident((SSM_WIDTH, 2 * SSM_LANES)), _resident((SSM_LANES, SSM_WIDTH)),
                  _resident((SSM_LANES, SSM_WIDTH)), _resident((1, SSM_WIDTH)),
                  _resident((SSM_WIDTH, 2 * D_MODEL))],
        out_specs=(pl.BlockSpec((nb, lc, D_MODEL), lambda c: (0, c, 0)),
                   pl.BlockSpec((nb, SSM_LANES), lambda c: (0, 0)), pl.BlockSpec((nb, SSM_LANES), lambda c: (0, 0))),
        out_shape=out_shape,
        scratch_shapes=[pltpu.VMEM((rows // SUBLANES, SUBLANES, 2 * SSM_LANES), F32),
                        pltpu.VMEM((SSM_WIDTH // LANES, rows, LANES), F32),
                        pltpu.VMEM((nb, SSM_LANES), F32), pltpu.VMEM((nb, SSM_LANES), F32)],
        compiler_params=_params(("arbitrary",)), name="ssm_prompt")(
            u_il, jnp.tile(are, (nb, 1)), jnp.tile(aim, (nb, 1)), bd, cre, cim, dsk, wglu)


def _ssm_step_kernel(u_ref, h0re_ref, h0im_ref, are_ref, aim_ref, bd_ref, cre_ref, cim_ref, dsk_ref, wglu_ref,
                     ys_ref, hre_ref, him_ref, h_s):
    u = u_ref[...]

    def store(lanes, val):
        h_s[:, lanes] = val

    _ssm_drive(u, bd_ref, store)
    ar, ai = are_ref[...], aim_ref[...]
    hr, hi = h0re_ref[...], h0im_ref[...]
    nhr = ar * hr - ai * hi + h_s[:, 0:SSM_LANES]
    nhi = ar * hi + ai * hr + h_s[:, SSM_LANES:2 * SSM_LANES]
    hre_ref[...] = nhr
    him_ref[...] = nhi
    h_s[:, 0:SSM_LANES] = nhr
    h_s[:, SSM_LANES:2 * SSM_LANES] = nhi
    yg = _ssm_readout(lambda lanes: h_s[:, lanes], u, cre_ref, cim_ref, dsk_ref)
    ys_ref[...] = _glu(yg.astype(BF16), wglu_ref).astype(BF16)


def _ssm_step(u, h0re, h0im, are, aim, bd, cre, cim, dsk, wglu):
    n = u.shape[0]
    out_shape = (jax.ShapeDtypeStruct((n, D_MODEL), BF16),
                 jax.ShapeDtypeStruct((n, SSM_LANES), F32), jax.ShapeDtypeStruct((n, SSM_LANES), F32))
    return pl.pallas_call(
        _ssm_step_kernel, out_shape=out_shape,
        scratch_shapes=[pltpu.VMEM((n, 2 * SSM_LANES), F32)],
        compiler_params=pltpu.CompilerParams(vmem_limit_bytes=VMEM_LIMIT), name="ssm_step")(
            u, h0re, h0im, are, aim, bd, cre, cim, dsk, wglu)


def _subln(o1, l1, o2, l2, lam, gain):
    o = o1 / l1 - lam * (o2 / l2)
    return _rms(o, gain) * (1.0 - LAM_INIT)


def _prompt_tile(qi, lam_ref, q_ref, k_ref, v_ref, gain_ref, o_ref, sa_s, sb_s, m_s, acc_s, tq, tk):
    lane = lax.broadcasted_iota(jnp.int32, (tk, QK_DIM), 1)
    parts = []
    for half in range(2):
        q = q_ref[half * tk:(half + 1) * tk, :]
        zero = jnp.zeros_like(q)
        parts += [jnp.where(lane < HEAD_DIM, q, zero), jnp.where(lane >= HEAD_DIM, q, zero)]
    qz = jnp.concatenate(parts, axis=0)
    m_s[...] = jnp.full_like(m_s, -jnp.inf)
    acc_s[...] = jnp.zeros_like(acc_s)
    ones = jnp.ones((tk, LANES), BF16)
    wide = lambda a: jnp.concatenate([a] * (tk // LANES), axis=1)

    def scores(c, s_ref, rows=slice(0, 2 * tq)):
        r0 = pl.multiple_of(c * tk, tk)
        s_ref[rows, :] = _dot_nt(qz[rows], k_ref[pl.ds(r0, tk), :])

    def softmax_pv(c, s_ref, rows=slice(0, 2 * tq), diag_rows=0):
        s = s_ref[rows, :]
        if diag_rows:
            row = lax.broadcasted_iota(jnp.int32, (diag_rows, tk), 0) % tk
            col = lax.broadcasted_iota(jnp.int32, (diag_rows, tk), 1)
            masked = jnp.where(col <= row, s[0:diag_rows], -jnp.inf)
            s = masked if diag_rows == s.shape[0] else jnp.concatenate([masked, s[diag_rows:]], axis=0)
        m_prev = m_s[rows, :]
        m_new = jnp.maximum(m_prev, jnp.max(s, axis=1, keepdims=True))
        alpha = jnp.exp2(m_prev - m_new)
        p = jnp.exp2(s - wide(m_new)).astype(BF16)
        r0 = pl.multiple_of(c * tk, tk)
        v1 = jnp.concatenate([v_ref[pl.ds(r0, tk), :], ones], axis=1)
        acc_s[rows, :] = jnp.concatenate([alpha, alpha], axis=1) * acc_s[rows, :] + _dot(p, v1)
        m_s[rows, :] = m_new

    scores(0, sa_s)

    def body(j, carry):
        c0 = 2 * j
        scores(c0 + 1, sb_s)
        softmax_pv(c0, sa_s)
        scores(c0 + 2, sa_s)
        softmax_pv(c0 + 1, sb_s)
        return carry

    lax.fori_loop(0, qi, body, 0)
    c0 = 2 * qi
    late = slice(tq, 2 * tq)
    scores(c0 + 1, sb_s, late)
    softmax_pv(c0, sa_s, diag_rows=tq)
    softmax_pv(c0 + 1, sb_s, late, diag_rows=tq)
    acc = acc_s[...]
    lam, gain = lam_ref[0, 0], gain_ref[...]
    for half in range(2):
        a1 = acc[2 * half * tk:(2 * half + 1) * tk]
        a2 = acc[(2 * half + 1) * tk:(2 * half + 2) * tk]
        o_ref[half * tk:(half + 1) * tk, :] = _subln(a1[:, 0:QK_DIM], a1[:, QK_DIM:2 * QK_DIM], a2[:, 0:QK_DIM],
                                                     a2[:, QK_DIM:2 * QK_DIM], lam, gain).astype(BF16)


def _decode_query(q_ref):
    q8 = q_ref[0].astype(F32)
    lane = lax.broadcasted_iota(jnp.int32, (N_HEADS, QK_DIM), 1)
    qz32 = jnp.concatenate([jnp.where(lane < HEAD_DIM, q8, 0.0), jnp.where(lane >= HEAD_DIM, q8, 0.0)], axis=0)
    return qz32, qz32.astype(BF16)


def _decode_groups(q_ref, k_groups, v_groups):
    n_maps = 2 * N_HEADS
    rows_pp = PAGE_SIZE * N_HEADS
    _, qz = _decode_query(q_ref)
    scores = [jnp.concatenate([_dot_nt(qz, r[...].astype(BF16)) for r in ks], axis=1) for ks in k_groups]
    stats, probs = [], []
    for s in scores:
        col = lax.broadcasted_iota(jnp.int32, s.shape, 1)
        row = lax.broadcasted_iota(jnp.int32, s.shape, 0)
        s = jnp.where(col % N_HEADS == row % N_HEADS, s, -jnp.inf)
        m = jnp.max(s, axis=1, keepdims=True)
        p = jnp.exp2(s - m)
        stats.append((m, jnp.sum(p, axis=1, keepdims=True)))
        probs.append(p.astype(BF16))
    out = []
    for (m, l), pb, vs in zip(stats, probs, v_groups):
        pv = _dot(pb[:, 0:rows_pp], vs[0][...].astype(BF16))
        for i in range(1, len(vs)):
            pv = pv + _dot(pb[:, i * rows_pp:(i + 1) * rows_pp], vs[i][...].astype(BF16))
        out.append((m, l, pv))
    return out


def _decode_finish(groups, lam_ref, gain_ref, q_ref, kn_ref, vn_ref, o_ref):
    n_maps = 2 * N_HEADS
    qz32, _ = _decode_query(q_ref)
    two = lambda a: jnp.concatenate([a, a], axis=0)
    s_n = jnp.sum(qz32 * two(kn_ref[0].astype(F32)), axis=1, keepdims=True)
    m = s_n
    for mg, _, _ in groups:
        m = jnp.maximum(m, mg)
    l = jnp.exp2(s_n - m)
    acc = l * two(vn_ref[0].astype(F32))
    for mg, lg, pvg in groups:
        w = jnp.exp2(mg - m)
        l = l + w * lg
        acc = acc + w * pvg
    o_ref[0] = _subln(acc[0:N_HEADS], l[0:N_HEADS], acc[N_HEADS:n_maps], l[N_HEADS:n_maps], lam_ref[0, 0],
                      gain_ref[...]).astype(BF16)


def _attn_kernel(pt_ref, lam_ref, q_ref, k_ref, v_ref, gain_ref, qs_ref, *refs, tq, tk, n_pages):
    del pt_ref
    k_refs, v_refs = refs[0:n_pages], refs[n_pages:2 * n_pages]
    kn_ref, vn_ref, o_ref, os_ref, sa_s, sb_s, m_s, acc_s = refs[2 * n_pages:]
    _prompt_tile(pl.program_id(2), lam_ref, q_ref, k_ref, v_ref, gain_ref, o_ref, sa_s, sb_s, m_s, acc_s, tq, tk)
    per = n_pages // DECODE_GROUPS
    split = lambda page_refs: [page_refs[g * per:(g + 1) * per] for g in range(DECODE_GROUPS)]
    groups = _decode_groups(qs_ref, split(k_refs), split(v_refs))
    _decode_finish(groups, lam_ref, gain_ref, qs_ref, kn_ref, vn_ref, os_ref)


def _attention(page_table, lam, q, k, v, gain, q_s, cache_k, cache_v, k_new, v_new, nb, t_len, tq):
    tk = tq // 2
    n, n_pages = page_table.shape
    nq = t_len // tq
    assert nb * N_HEADS * nq == n, "one decode sample per prompt grid step"
    kern = functools.partial(_attn_kernel, tq=tq, tk=tk, n_pages=n_pages)
    sample = lambda b, h, i: (b * N_HEADS + h) * nq + i
    qspec = pl.BlockSpec((None, tq, QK_DIM), lambda b, h, i, pt: (b, i, h))
    kvspec = pl.BlockSpec((None, t_len, QK_DIM), lambda b, h, i, pt: (b, 0, h))
    row = pl.BlockSpec((1, N_HEADS, QK_DIM), lambda b, h, i, pt: (sample(b, h, i), 0, 0))

    def page(j):
        return pl.BlockSpec((None, PAGE_SIZE * N_HEADS, QK_DIM), lambda b, h, i, pt: (pt[sample(b, h, i), j], 0, 0))

    pages = [page(j) for j in range(n_pages)]
    grid_spec = pltpu.PrefetchScalarGridSpec(
        num_scalar_prefetch=1, grid=(nb, N_HEADS, nq),
        in_specs=[pl.BlockSpec(memory_space=pltpu.SMEM), qspec, kvspec, kvspec,
                  pl.BlockSpec((1, QK_DIM), lambda b, h, i, pt: (0, 0)), row] + pages + pages + [row, row],
        out_specs=(qspec, row),
        scratch_shapes=[pltpu.VMEM((2 * tq, tk), F32), pltpu.VMEM((2 * tq, tk), F32),
                        pltpu.VMEM((2 * tq, LANES), F32), pltpu.VMEM((2 * tq, 2 * QK_DIM), F32)])
    r3 = lambda a: a.reshape(n, N_HEADS, QK_DIM)
    out_shape = (jax.ShapeDtypeStruct((nb, t_len, ATTN_WIDTH), BF16), jax.ShapeDtypeStruct((n, N_HEADS, QK_DIM), BF16))
    return pl.pallas_call(
        kern, grid_spec=grid_spec, out_shape=out_shape,
        compiler_params=_params(("arbitrary", "arbitrary", "arbitrary"), ATTN_VMEM_LIMIT), name="attention")(
            page_table, lam, q, k, v, gain, r3(q_s), *([cache_k] * n_pages), *([cache_v] * n_pages),
            r3(k_new), r3(v_new))


FF_CHUNKS = ((0, 1024), (1024, 2048), (2048, D_FF))


def _out_ffn_kernel(x_ref, ys_ref, o_ref, gs_ref, ga_ref, wo_ref, npm_ref, npf_ref, wg_ref, wu_ref, wd_ref, nof_ref,
                    y_ref):
    subs = _row_subtiles(x_ref.shape[0])
    f32 = lambda ref, rows: ref[rows, :].astype(F32)
    mixed = [(f32(gs_ref, r) * f32(ys_ref, r) + f32(ga_ref, r) * f32(o_ref, r)).astype(BF16) for r in subs]
    proj = [_dot(m, wo_ref[...]) for m in mixed]
    x1 = [x_ref[r, :] + _rms(p, npm_ref[...]) for r, p in zip(subs, proj)]
    hf = [_rms(x, npf_ref[...]).astype(BF16) for x in x1]
    f = [None] * len(subs)
    for c0, c1 in FF_CHUNKS:
        for i, h in enumerate(hf):
            act = (jax.nn.silu(_dot(h, wg_ref[:, c0:c1])) * _dot(h, wu_ref[:, c0:c1])).astype(BF16)
            part = _dot(act, wd_ref[c0:c1, :])
            f[i] = part if f[i] is None else f[i] + part
    for r, x, fi in zip(subs, x1, f):
        y_ref[r, :] = x + _rms(fi, nof_ref[...])


def _out_ffn(x2d, ys, o, gs, ga, wo, npm, npf, wg, wu, wd, nof, tm):
    m = x2d.shape[0]
    row = pl.BlockSpec((tm, D_MODEL), lambda i: (i, 0))
    vec = _resident((1, D_MODEL))
    return pl.pallas_call(
        _out_ffn_kernel, grid=(m // tm,),
        in_specs=[row, row, row, row, row, _resident((D_MODEL, D_MODEL)), vec, vec,
                  _resident((D_MODEL, D_FF)), _resident((D_MODEL, D_FF)), _resident((D_FF, D_MODEL)), vec],
        out_specs=row, out_shape=jax.ShapeDtypeStruct((m, D_MODEL), F32),
        compiler_params=_params(("parallel",)), name="out_ffn")(x2d, ys, o, gs, ga, wo, npm, npf, wg, wu, wd, nof)


def kernel(x_prompt, x_sample, cache_k, cache_v, state_ssm_re, state_ssm_im, page_table, norm_pre_mix, w_in,
           ssm_lambda_re, ssm_lambda_im, ssm_log_dt, ssm_b_re, ssm_b_im, ssm_c_re, ssm_c_im, ssm_d, w_glu_a,
           w_glu_b, lambda_q1, lambda_k1, lambda_q2, lambda_k2, subln_gain, w_o, norm_post_mix, norm_pre_ffn,
           w_gate, w_up, w_down, norm_post_ffn):
    assert w_in.shape[0] == 1, "single-layer stack"
    bp, tp, _ = x_prompt.shape
    bs, ts, _ = x_sample.shape
    assert ts == 1

    are, aim, bbre, bbim, lam = _prep(ssm_lambda_re[0], ssm_lambda_im[0], ssm_log_dt[0], ssm_b_re[0], ssm_b_im[0],
                                      lambda_q1[0], lambda_k1[0], lambda_q2[0], lambda_k2[0])
    are, aim = are.reshape(1, SSM_LANES), aim.reshape(1, SSM_LANES)
    bd = jnp.concatenate([_block_diag(bbre), _block_diag(bbim)], axis=1).astype(BF16)
    cre = _block_diag(jnp.swapaxes(ssm_c_re[0], 1, 2)).astype(BF16)
    cim = _block_diag(jnp.swapaxes(-ssm_c_im[0], 1, 2)).astype(BF16)
    dsk = ssm_d[0].reshape(1, SSM_WIDTH)
    wglu = jnp.concatenate([w_glu_a[0], w_glu_b[0]], axis=1).astype(BF16)
    vec = lambda a: a[0].reshape(1, -1)
    w_in_bf, wo_bf = w_in[0].astype(BF16), w_o[0].astype(BF16)
    wg_bf, wu_bf, wd_bf = w_gate[0].astype(BF16), w_up[0].astype(BF16), w_down[0].astype(BF16)
    gain = vec(subln_gain)
    ssm_w = (are, aim, bd, cre, cim, dsk, wglu)
    ffn_w = (wo_bf, vec(norm_post_mix), vec(norm_pre_ffn), wg_bf, wu_bf, wd_bf, vec(norm_post_ffn))

    xp = x_prompt.reshape(bp * tp, D_MODEL)
    u_il, q, kf, kb, vf, vb, gs, ga = _in_proj(xp, np.arange(tp), bp, vec(norm_pre_mix), w_in_bf, tm=512)
    ys, hpr, hpi = _ssm_prompt(u_il, bp, tp, *ssm_w, lc=128)

    xs = x_sample.reshape(bs, D_MODEL)
    pos_s = np.full((bs,), PAST_LEN)
    u_s, q_s, kf_s, kb_s, vf_s, vb_s, gs_s, ga_s = _in_proj(xs, pos_s, 1, vec(norm_pre_mix), w_in_bf, tm=bs)
    ys_s, hsr, hsi = _ssm_step(u_s, state_ssm_re[0].reshape(bs, SSM_LANES), state_ssm_im[0].reshape(bs, SSM_LANES),
                               *ssm_w)

    n_phys = cache_k.shape[1]
    pages = lambda c: c[0].reshape(n_phys, PAGE_SIZE * N_HEADS, QK_DIM)
    r3 = lambda a: a.reshape(bp, tp, ATTN_WIDTH)
    o, o_s = _attention(page_table, lam, r3(q), r3(kb), r3(vb), gain, q_s, pages(cache_k), pages(cache_v),
                        kb_s, vb_s, bp, tp, tq=1024)

    yp = _out_ffn(xp, ys.reshape(bp * tp, D_MODEL), o.reshape(bp * tp, ATTN_WIDTH), gs, ga, *ffn_w, tm=512)
    ysamp = _out_ffn(xs, ys_s, o_s.reshape(bs, ATTN_WIDTH), gs_s, ga_s, *ffn_w, tm=bs)

    st = lambda a, n: a.reshape(1, n, SSM_GROUPS, SSM_STATE)
    kv = lambda a, n, t: a.reshape(1, n, t, N_HEADS, QK_DIM)
    return (yp.reshape(bp, tp, D_MODEL), ysamp.reshape(bs, ts, D_MODEL),
            kv(kf, bp, tp), kv(vf, bp, tp), st(hpr, bp), st(hpi, bp),
            kv(kf_s, bs, ts), kv(vf_s, bs, ts), st(hsr, bs), st(hsi, bs))
```

```python
import functools
import math

import jax
import jax.numpy as jnp
import numpy as np
from jax import lax
from jax.experimental import pallas as pl
from jax.experimental.pallas import tpu as pltpu

F32 = jnp.float32
BF16 = jnp.bfloat16

D_MODEL = 1024
PAST_LEN = 2048
PAGE_SIZE = 128
SSM_WIDTH = 512
SSM_GROUP = 16
SSM_GROUPS = 32
SSM_STATE = 64
SSM_LANES = SSM_GROUPS * SSM_STATE
HEAD_DIM = 64
N_HEADS = 8
QK_DIM = 128
ATTN_WIDTH = 1024
ROPE_DIM = 16
ROPE_THETA = 500000.0
D_FF = 2816
IN_WIDTH = SSM_WIDTH + 3 * ATTN_WIDTH + 2 * D_MODEL
EPS = 1e-6
LAM_INIT = 0.8 - 0.6 * math.exp(-0.3 * 0)
Q_SCALE = HEAD_DIM ** -0.5 * math.log2(math.e)

LANES = 128
SUBLANES = 8
MXU_WIDTH = 256
VMEM_LIMIT = 56 * 1024 * 1024
ATTN_VMEM_LIMIT = 62 * 1024 * 1024

PROJ_ROWS = 512
SSM_CHUNK = 128
ATTN_Q_ROWS = 1024
DECODE_GROUPS = 2
ROW_SUBTILES = 2
MIN_SUBTILE_ROWS = MXU_WIDTH


def _params(sem, vmem=VMEM_LIMIT):
    return pltpu.CompilerParams(dimension_semantics=sem, vmem_limit_bytes=vmem)


def _row_subtiles(tm):
    sub = max(tm // ROW_SUBTILES, min(tm, MIN_SUBTILE_ROWS))
    return [slice(r0, r0 + sub) for r0 in range(0, tm, sub)]


def _resident(shape):
    return pl.BlockSpec(shape, lambda *_: (0,) * len(shape), pipeline_mode=pl.Buffered(1))


def _rms(x, g):
    return x * lax.rsqrt(jnp.mean(x * x, axis=-1, keepdims=True) + EPS) * g


def _dot(a, b):
    return jnp.dot(a, b, preferred_element_type=F32)


def _dot_nt(a, b):
    return lax.dot_general(a, b, (((1,), (1,)), ((), ())), preferred_element_type=F32)


def _prep_kernel(lre_ref, lim_ref, ldt_ref, bre_ref, bim_ref, lq1_ref, lk1_ref, lq2_ref, lk2_ref,
                 are_ref, aim_ref, bbre_ref, bbim_ref, lam_ref):
    lre, lim = lre_ref[...], lim_ref[...]
    dt = jnp.exp(ldt_ref[...])
    mag = jnp.exp(lre * dt)
    ar, ai = mag * jnp.cos(lim * dt), mag * jnp.sin(lim * dt)
    den = lre * lre + lim * lim
    fr = ((ar - 1.0) * lre + ai * lim) / den
    fi = (ai * lre - (ar - 1.0) * lim) / den
    are_ref[...] = ar
    aim_ref[...] = ai
    bre, bim = bre_ref[...], bim_ref[...]
    bbre_ref[...] = fr[:, None, :] * bre - fi[:, None, :] * bim
    bbim_ref[...] = fr[:, None, :] * bim + fi[:, None, :] * bre
    s1 = jnp.sum(lq1_ref[...] * lk1_ref[...], axis=-1, keepdims=True)
    s2 = jnp.sum(lq2_ref[...] * lk2_ref[...], axis=-1, keepdims=True)
    lam_ref[...] = jnp.exp(s1) - jnp.exp(s2) + LAM_INIT


def _prep(lam_re, lam_im, log_dt, b_re, b_im, lq1, lk1, lq2, lk2):
    g, p, c = SSM_GROUPS, SSM_STATE, SSM_GROUP
    out_shape = (jax.ShapeDtypeStruct((g, p), F32), jax.ShapeDtypeStruct((g, p), F32),
                 jax.ShapeDtypeStruct((g, c, p), F32), jax.ShapeDtypeStruct((g, c, p), F32),
                 jax.ShapeDtypeStruct((1, 1), F32))
    return pl.pallas_call(_prep_kernel, out_shape=out_shape, name="ssm_prep")(
        lam_re, lam_im, log_dt.reshape(g, 1), jnp.swapaxes(b_re, 1, 2), jnp.swapaxes(b_im, 1, 2),
        lq1.reshape(1, HEAD_DIM), lk1.reshape(1, HEAD_DIM), lq2.reshape(1, HEAD_DIM), lk2.reshape(1, HEAD_DIM))


def _block_diag(blocks):
    g, r, c = blocks.shape
    keep = np.arange(g * r)[:, None] // r == np.arange(g * c)[None, :] // c
    return jnp.where(keep, jnp.tile(blocks.reshape(g * r, c), (1, g)), 0.0)


def _in_proj_kernel(x_ref, g_ref, w_ref, cos_ref, sa_ref, sb_ref,
                    u_ref, q_ref, kf_ref, kb_ref, vf_ref, vb_ref, gs_ref, ga_ref):
    o0 = SSM_WIDTH
    o1, o2, o3, o4 = o0 + ATTN_WIDTH, o0 + 2 * ATTN_WIDTH, o0 + 3 * ATTN_WIDTH, o0 + 3 * ATTN_WIDTH + D_MODEL
    subs = _row_subtiles(x_ref.shape[0])
    sub = subs[0].stop
    heads = [slice(hh * QK_DIM, (hh + 1) * QK_DIM) for hh in range(N_HEADS)]

    def rope(z, rows):
        return (z * cos_ref[rows, :] + pltpu.roll(z, LANES - ROPE_DIM // 2, 1) * sa_ref[rows, :]
                + pltpu.roll(z, ROPE_DIM // 2, 1) * sb_ref[rows, :])

    def seg_u(rows, h):
        u_ref[rows, :] = _dot(h, w_ref[:, 0:o0]).astype(BF16)

    def seg_q(rows, h):
        zq = _dot(h, w_ref[:, o0:o1])
        for sl in heads:
            q_ref[rows, sl] = (rope(zq[:, sl], rows) * Q_SCALE).astype(BF16)

    def seg_k(rows, h):
        zk = _dot(h, w_ref[:, o1:o2])
        for hh, sl in enumerate(heads):
            kr = rope(zk[:, sl], rows)
            kf_ref[pl.ds(rows.start * N_HEADS + hh, sub, stride=N_HEADS), :] = kr
            kb_ref[rows, sl] = kr.astype(BF16)

    def seg_v(rows, h):
        zv = _dot(h, w_ref[:, o2:o3])
        for hh, sl in enumerate(heads):
            vf_ref[pl.ds(rows.start * N_HEADS + hh, sub, stride=N_HEADS), :] = zv[:, sl]
        vb_ref[rows, :] = zv.astype(BF16)

    def seg_gs(rows, h):
        gs_ref[rows, :] = jax.nn.sigmoid(_dot(h, w_ref[:, o3:o4])).astype(BF16)

    def seg_ga(rows, h):
        ga_ref[rows, :] = jax.nn.sigmoid(_dot(h, w_ref[:, o4:IN_WIDTH])).astype(BF16)

    hs = [_rms(x_ref[rows, :], g_ref[...]).astype(BF16) for rows in subs]
    for seg in (seg_u, seg_q, seg_k, seg_v, seg_gs, seg_ga):
        for rows, h in zip(subs, hs):
            seg(rows, h)


def _rope_tables(pos):
    half = ROPE_DIM // 2
    inv = ROPE_THETA ** (-np.arange(half, dtype=np.float64) * 2.0 / ROPE_DIM)
    ang = np.asarray(pos, np.float64)[:, None] * inv[None, :]
    cos, sin = np.cos(ang), np.sin(ang)
    n = ang.shape[0]
    comp_cos = np.concatenate([cos, cos, np.ones((n, HEAD_DIM - ROPE_DIM))], axis=1)
    comp_sa = np.concatenate([-sin, np.zeros((n, HEAD_DIM - half))], axis=1)
    comp_sb = np.concatenate([np.zeros((n, half)), sin, np.zeros((n, HEAD_DIM - ROPE_DIM))], axis=1)
    two = lambda t: jnp.asarray(np.concatenate([t, t], axis=1), F32)
    return two(comp_cos), two(comp_sa), two(comp_sb)


def _in_proj(x2d, pos, nb, g_pre, w_in_bf, tm):
    m = x2d.shape[0]
    t_len = m // nb
    nt = t_len // tm
    cos, sa, sb = _rope_tables(pos)
    row = lambda w: pl.BlockSpec((tm, w), lambda i: (i, 0))
    heads = pl.BlockSpec((tm * N_HEADS, QK_DIM), lambda i: (i, 0))
    tab = pl.BlockSpec((tm, LANES), lambda i: (i % nt, 0))
    out_shape = (jax.ShapeDtypeStruct((t_len, nb * SSM_WIDTH), BF16),
                 jax.ShapeDtypeStruct((m, ATTN_WIDTH), BF16),
                 jax.ShapeDtypeStruct((m * N_HEADS, QK_DIM), F32), jax.ShapeDtypeStruct((m, ATTN_WIDTH), BF16),
                 jax.ShapeDtypeStruct((m * N_HEADS, QK_DIM), F32), jax.ShapeDtypeStruct((m, ATTN_WIDTH), BF16),
                 jax.ShapeDtypeStruct((m, D_MODEL), BF16), jax.ShapeDtypeStruct((m, D_MODEL), BF16))
    out_specs = (pl.BlockSpec((tm, SSM_WIDTH), lambda i: (i % nt, i // nt)),
                 row(ATTN_WIDTH), heads, row(ATTN_WIDTH), heads, row(ATTN_WIDTH),
                 row(D_MODEL), row(D_MODEL))
    return pl.pallas_call(
        _in_proj_kernel, grid=(m // tm,),
        in_specs=[row(D_MODEL), _resident((1, D_MODEL)), _resident((D_MODEL, IN_WIDTH)), tab, tab, tab],
        out_specs=out_specs, out_shape=out_shape,
        compiler_params=_params(("parallel",)), name="in_proj")(x2d, g_pre, w_in_bf, cos, sa, sb)


def _ssm_drive(u, bd_ref, store):
    n_tiles = 2 * SSM_LANES // MXU_WIDTH
    for j in range(n_tiles):
        ks = LANES * ((j % (n_tiles // 2)) // 2)
        lanes = slice(MXU_WIDTH * j, MXU_WIDTH * (j + 1))
        store(lanes, _dot(u[:, ks:ks + LANES], bd_ref[ks:ks + LANES, lanes]))


def _ssm_readout(load, u, cre_ref, cim_ref, dsk_ref):
    half = SSM_LANES // 2
    ys = []
    for j in range(2):
        hre = load(slice(half * j, half * (j + 1))).astype(BF16)
        him = load(slice(SSM_LANES + half * j, SSM_LANES + half * (j + 1))).astype(BF16)
        cols = slice(MXU_WIDTH * j, MXU_WIDTH * (j + 1))
        ys.append(_dot(hre, cre_ref[half * j:half * (j + 1), cols]) + _dot(him, cim_ref[half * j:half * (j + 1), cols]))
    y = jnp.concatenate(ys, axis=1) + dsk_ref[...] * u.astype(F32)
    return jax.nn.gelu(y)


def _glu(yg, wglu_ref):
    return _dot(yg, wglu_ref[:, 0:D_MODEL]) * jax.nn.sigmoid(_dot(yg, wglu_ref[:, D_MODEL:2 * D_MODEL]))


def _ssm_prompt_kernel(u_ref, are_ref, aim_ref, bd_ref, cre_ref, cim_ref, dsk_ref, wglu_ref,
                       ys_ref, hre_ref, him_ref, h_s, il_s, st_re, st_im, *, nb, lc):
    @pl.when(pl.program_id(0) == 0)
    def _():
        st_re[...] = jnp.zeros_like(st_re)
        st_im[...] = jnp.zeros_like(st_im)

    rows = lc * nb
    n_tr = rows // SUBLANES
    n_slabs = SSM_WIDTH // LANES
    slab = lambda j: slice(j * LANES, (j + 1) * LANES)
    seq_rows = lambda b: pl.ds(b, lc, stride=nb)
    for b in range(nb):
        ub = u_ref[:, b * SSM_WIDTH:(b + 1) * SSM_WIDTH].astype(F32)
        for j in range(n_slabs):
            il_s[j, seq_rows(b), :] = ub[:, slab(j)]
    u = jnp.concatenate([il_s[j] for j in range(n_slabs)], axis=1).astype(BF16)

    def store(lanes, val):
        h_s[:, :, lanes] = val.reshape(n_tr, SUBLANES, val.shape[-1])

    _ssm_drive(u, bd_ref, store)

    ar, ai = are_ref[...], aim_ref[...]
    hr, hi = st_re[...], st_im[...]
    for s in range(n_tr):
        for k in range(SUBLANES // nb):
            sub = slice(k * nb, (k + 1) * nb)
            nhr = ar * hr - ai * hi + h_s[s, sub, 0:SSM_LANES]
            nhi = ar * hi + ai * hr + h_s[s, sub, SSM_LANES:2 * SSM_LANES]
            h_s[s, sub, 0:SSM_LANES] = nhr
            h_s[s, sub, SSM_LANES:2 * SSM_LANES] = nhi
            hr, hi = nhr, nhi
    st_re[...] = hr
    st_im[...] = hi
    hre_ref[...] = hr
    him_ref[...] = hi

    load = lambda lanes: h_s[:, :, lanes].reshape(rows, lanes.stop - lanes.start)
    yg = _ssm_readout(load, u, cre_ref, cim_ref, dsk_ref)
    for j in range(n_slabs):
        il_s[j] = yg[:, slab(j)]
    per_seq = [jnp.concatenate([il_s[j, seq_rows(b), :] for j in range(n_slabs)], axis=1) for b in range(nb)]
    res = _glu(jnp.concatenate(per_seq, axis=0).astype(BF16), wglu_ref)
    for b in range(nb):
        ys_ref[b] = res[b * lc:(b + 1) * lc].astype(BF16)


def _ssm_prompt(u_il, nb, t_len, are, aim, bd, cre, cim, dsk, wglu, lc):
    rows = lc * nb
    kern = functools.partial(_ssm_prompt_kernel, nb=nb, lc=lc)
    out_shape = (jax.ShapeDtypeStruct((nb, t_len, D_MODEL), BF16),
                 jax.ShapeDtypeStruct((nb, SSM_LANES), F32), jax.ShapeDtypeStruct((nb, SSM_LANES), F32))
    return pl.pallas_call(
        kern, grid=(t_len // lc,),
        in_specs=[pl.BlockSpec((lc, nb * SSM_WIDTH), lambda c: (c, 0)),
                  _resident((nb, SSM_LANES)), _resident((nb, SSM_LANES)),
                  _resident((SSM_WIDTH, 2 * SSM_LANES)), _resident((SSM_LANES, SSM_WIDTH)),
                  _resident((SSM_LANES, SSM_WIDTH)), _resident((1, SSM_WIDTH)),
                  _resident((SSM_WIDTH, 2 * D_MODEL))],
        out_specs=(pl.BlockSpec((nb, lc, D_MODEL), lambda c: (0, c, 0)),
                   pl.BlockSpec((nb, SSM_LANES), lambda c: (0, 0)), pl.BlockSpec((nb, SSM_LANES), lambda c: (0, 0))),
        out_shape=out_shape,
        scratch_shapes=[pltpu.VMEM((rows // SUBLANES, SUBLANES, 2 * SSM_LANES), F32),
                        pltpu.VMEM((SSM_WIDTH // LANES, rows, LANES), F32),
                        pltpu.VMEM((nb, SSM_LANES), F32), pltpu.VMEM((nb, SSM_LANES), F32)],
        compiler_params=_params(("arbitrary",)), name="ssm_prompt")(
            u_il, jnp.tile(are, (nb, 1)), jnp.tile(aim, (nb, 1)), bd, cre, cim, dsk, wglu)


def _ssm_step_kernel(u_ref, h0re_ref, h0im_ref, are_ref, aim_ref, bd_ref, cre_ref, cim_ref, dsk_ref, wglu_ref,
                     ys_ref, hre_ref, him_ref, h_s):
    u = u_ref[...]

    def store(lanes, val):
        h_s[:, lanes] = val

    _ssm_drive(u, bd_ref, store)
    ar, ai = are_ref[...], aim_ref[...]
    hr, hi = h0re_ref[...], h0im_ref[...]
    nhr = ar * hr - ai * hi + h_s[:, 0:SSM_LANES]
    nhi = ar * hi + ai * hr + h_s[:, SSM_LANES:2 * SSM_LANES]
    hre_ref[...] = nhr
    him_ref[...] = nhi
    h_s[:, 0:SSM_LANES] = nhr
    h_s[:, SSM_LANES:2 * SSM_LANES] = nhi
    yg = _ssm_readout(lambda lanes: h_s[:, lanes], u, cre_ref, cim_ref, dsk_ref)
    ys_ref[...] = _glu(yg.astype(BF16), wglu_ref).astype(BF16)


def _ssm_step(u, h0re, h0im, are, aim, bd, cre, cim, dsk, wglu):
    n = u.shape[0]
    out_shape = (jax.ShapeDtypeStruct((n, D_MODEL), BF16),
                 jax.ShapeDtypeStruct((n, SSM_LANES), F32), jax.ShapeDtypeStruct((n, SSM_LANES), F32))
    return pl.pallas_call(
        _ssm_step_kernel, out_shape=out_shape,
        scratch_shapes=[pltpu.VMEM((n, 2 * SSM_LANES), F32)],
        compiler_params=pltpu.CompilerParams(vmem_limit_bytes=VMEM_LIMIT), name="ssm_step")(
            u, h0re, h0im, are, aim, bd, cre, cim, dsk, wglu)


def _subln(o1, l1, o2, l2, lam, gain):
    o = o1 / l1 - lam * (o2 / l2)
    return _rms(o, gain) * (1.0 - LAM_INIT)


def _prompt_tile(qi, lam_ref, q_ref, k_ref, v_ref, gain_ref, o_ref, sa_s, sb_s, m_s, acc_s, tq, tk):
    lane = lax.broadcasted_iota(jnp.int32, (tk, QK_DIM), 1)
    parts = []
    for half in range(2):
        q = q_ref[half * tk:(half + 1) * tk, :]
        zero = jnp.zeros_like(q)
        parts += [jnp.where(lane < HEAD_DIM, q, zero), jnp.where(lane >= HEAD_DIM, q, zero)]
    qz = jnp.concatenate(parts, axis=0)
    m_s[...] = jnp.full_like(m_s, -jnp.inf)
    acc_s[...] = jnp.zeros_like(acc_s)
    ones = jnp.ones((tk, LANES), BF16)
    wide = lambda a: jnp.concatenate([a] * (tk // LANES), axis=1)

    def scores(c, s_ref, rows=slice(0, 2 * tq)):
        s_ref[rows, :] = _dot_nt(qz[rows], k_ref[c * tk:(c + 1) * tk, :])

    def softmax_pv(c, s_ref, rows=slice(0, 2 * tq), diag_rows=0):
        s = s_ref[rows, :]
        if diag_rows:
            row = lax.broadcasted_iota(jnp.int32, (diag_rows, tk), 0) % tk
            col = lax.broadcasted_iota(jnp.int32, (diag_rows, tk), 1)
            masked = jnp.where(col <= row, s[0:diag_rows], -jnp.inf)
            s = masked if diag_rows == s.shape[0] else jnp.concatenate([masked, s[diag_rows:]], axis=0)
        m_prev = m_s[rows, :]
        m_new = jnp.maximum(m_prev, jnp.max(s, axis=1, keepdims=True))
        alpha = jnp.exp2(m_prev - m_new)
        p = jnp.exp2(s - wide(m_new)).astype(BF16)
        v1 = jnp.concatenate([v_ref[c * tk:(c + 1) * tk, :], ones], axis=1)
        acc_s[rows, :] = jnp.concatenate([alpha, alpha], axis=1) * acc_s[rows, :] + _dot(p, v1)
        m_s[rows, :] = m_new

    scores(0, sa_s)
    for j in range(qi):
        c0 = 2 * j
        scores(c0 + 1, sb_s)
        softmax_pv(c0, sa_s)
        scores(c0 + 2, sa_s)
        softmax_pv(c0 + 1, sb_s)
    c0 = 2 * qi
    late = slice(tq, 2 * tq)
    scores(c0 + 1, sb_s, late)
    softmax_pv(c0, sa_s, diag_rows=tq)
    softmax_pv(c0 + 1, sb_s, late, diag_rows=tq)
    acc = acc_s[...]
    lam, gain = lam_ref[0, 0], gain_ref[...]
    for half in range(2):
        a1 = acc[2 * half * tk:(2 * half + 1) * tk]
        a2 = acc[(2 * half + 1) * tk:(2 * half + 2) * tk]
        o_ref[half * tk:(half + 1) * tk, :] = _subln(a1[:, 0:QK_DIM], a1[:, QK_DIM:2 * QK_DIM], a2[:, 0:QK_DIM],
                                                     a2[:, QK_DIM:2 * QK_DIM], lam, gain).astype(BF16)


def _decode_query(q_ref):
    q8 = q_ref[0].astype(F32)
    lane = lax.broadcasted_iota(jnp.int32, (N_HEADS, QK_DIM), 1)
    qz32 = jnp.concatenate([jnp.where(lane < HEAD_DIM, q8, 0.0), jnp.where(lane >= HEAD_DIM, q8, 0.0)], axis=0)
    return qz32, qz32.astype(BF16)


def _decode_groups(q_ref, k_groups, v_groups):
    n_maps = 2 * N_HEADS
    rows_pp = PAGE_SIZE * N_HEADS
    _, qz = _decode_query(q_ref)
    scores = [jnp.concatenate([_dot_nt(qz, r[...].astype(BF16)) for r in ks], axis=1) for ks in k_groups]
    stats, probs = [], []
    for s in scores:
        col = lax.broadcasted_iota(jnp.int32, s.shape, 1)
        row = lax.broadcasted_iota(jnp.int32, s.shape, 0)
        s = jnp.where(col % N_HEADS == row % N_HEADS, s, -jnp.inf)
        m = jnp.max(s, axis=1, keepdims=True)
        p = jnp.exp2(s - m)
        stats.append((m, jnp.sum(p, axis=1, keepdims=True)))
        probs.append(p.astype(BF16))
    out = []
    for (m, l), pb, vs in zip(stats, probs, v_groups):
        pv = _dot(pb[:, 0:rows_pp], vs[0][...].astype(BF16))
        for i in range(1, len(vs)):
            pv = pv + _dot(pb[:, i * rows_pp:(i + 1) * rows_pp], vs[i][...].astype(BF16))
        out.append((m, l, pv))
    return out


def _decode_finish(groups, lam_ref, gain_ref, q_ref, kn_ref, vn_ref, o_ref):
    n_maps = 2 * N_HEADS
    qz32, _ = _decode_query(q_ref)
    two = lambda a: jnp.concatenate([a, a], axis=0)
    s_n = jnp.sum(qz32 * two(kn_ref[0].astype(F32)), axis=1, keepdims=True)
    m = s_n
    for mg, _, _ in groups:
        m = jnp.maximum(m, mg)
    l = jnp.exp2(s_n - m)
    acc = l * two(vn_ref[0].astype(F32))
    for mg, lg, pvg in groups:
        w = jnp.exp2(mg - m)
        l = l + w * lg
        acc = acc + w * pvg
    o_ref[0] = _subln(acc[0:N_HEADS], l[0:N_HEADS], acc[N_HEADS:n_maps], l[N_HEADS:n_maps], lam_ref[0, 0],
                      gain_ref[...]).astype(BF16)


def _attn_kernel(pt_ref, lam_ref, q_ref, k_ref, v_ref, gain_ref, qs_ref, *refs, tq, tk, n_pages):
    del pt_ref
    k_refs, v_refs = refs[0:n_pages], refs[n_pages:2 * n_pages]
    kn_ref, vn_ref, o_ref, os_ref, sa_s, sb_s, m_s, acc_s = refs[2 * n_pages:]
    per = n_pages // DECODE_GROUPS
    split = lambda page_refs: [page_refs[g * per:(g + 1) * per] for g in range(DECODE_GROUPS)]
    for qi in range(k_ref.shape[0] // tq):
        @pl.when(pl.program_id(2) == qi)
        def _(qi=qi):
            _prompt_tile(qi, lam_ref, q_ref, k_ref, v_ref, gain_ref, o_ref, sa_s, sb_s, m_s, acc_s, tq, tk)
            groups = _decode_groups(qs_ref, split(k_refs), split(v_refs))
            _decode_finish(groups, lam_ref, gain_ref, qs_ref, kn_ref, vn_ref, os_ref)


def _attention(page_table, lam, q, k, v, gain, q_s, cache_k, cache_v, k_new, v_new, nb, t_len, tq):
    tk = tq // 2
    n, n_pages = page_table.shape
    nq = t_len // tq
    assert nb * N_HEADS * nq == n, "one decode sample per prompt grid step"
    kern = functools.partial(_attn_kernel, tq=tq, tk=tk, n_pages=n_pages)
    sample = lambda b, h, i: (b * N_HEADS + h) * nq + i
    qspec = pl.BlockSpec((None, tq, QK_DIM), lambda b, h, i, pt: (b, i, h))
    kvspec = pl.BlockSpec((None, t_len, QK_DIM), lambda b, h, i, pt: (b, 0, h))
    row = pl.BlockSpec((1, N_HEADS, QK_DIM), lambda b, h, i, pt: (sample(b, h, i), 0, 0))

    def page(j):
        return pl.BlockSpec((None, PAGE_SIZE * N_HEADS, QK_DIM), lambda b, h, i, pt: (pt[sample(b, h, i), j], 0, 0))

    pages = [page(j) for j in range(n_pages)]
    grid_spec = pltpu.PrefetchScalarGridSpec(
        num_scalar_prefetch=1, grid=(nb, N_HEADS, nq),
        in_specs=[pl.BlockSpec(memory_space=pltpu.SMEM), qspec, kvspec, kvspec,
                  pl.BlockSpec((1, QK_DIM), lambda b, h, i, pt: (0, 0)), row] + pages + pages + [row, row],
        out_specs=(qspec, row),
        scratch_shapes=[pltpu.VMEM((2 * tq, tk), F32), pltpu.VMEM((2 * tq, tk), F32),
                        pltpu.VMEM((2 * tq, LANES), F32), pltpu.VMEM((2 * tq, 2 * QK_DIM), F32)])
    r3 = lambda a: a.reshape(n, N_HEADS, QK_DIM)
    out_shape = (jax.ShapeDtypeStruct((nb, t_len, ATTN_WIDTH), BF16), jax.ShapeDtypeStruct((n, N_HEADS, QK_DIM), BF16))
    return pl.pallas_call(
        kern, grid_spec=grid_spec, out_shape=out_shape,
        compiler_params=_params(("arbitrary", "arbitrary", "arbitrary"), ATTN_VMEM_LIMIT), name="attention")(
            page_table, lam, q, k, v, gain, r3(q_s), *([cache_k] * n_pages), *([cache_v] * n_pages),
            r3(k_new), r3(v_new))


FF_CHUNKS = ((0, 1024), (1024, 2048), (2048, D_FF))


def _out_ffn_kernel(x_ref, ys_ref, o_ref, gs_ref, ga_ref, wo_ref, npm_ref, npf_ref, wg_ref, wu_ref, wd_ref, nof_ref,
                    y_ref):
    subs = _row_subtiles(x_ref.shape[0])
    f32 = lambda ref, rows: ref[rows, :].astype(F32)
    mixed = [(f32(gs_ref, r) * f32(ys_ref, r) + f32(ga_ref, r) * f32(o_ref, r)).astype(BF16) for r in subs]
    proj = [_dot(m, wo_ref[...]) for m in mixed]
    x1 = [x_ref[r, :] + _rms(p, npm_ref[...]) for r, p in zip(subs, proj)]
    hf = [_rms(x, npf_ref[...]).astype(BF16) for x in x1]
    f = [None] * len(subs)
    for c0, c1 in FF_CHUNKS:
        for i, h in enumerate(hf):
            act = (jax.nn.silu(_dot(h, wg_ref[:, c0:c1])) * _dot(h, wu_ref[:, c0:c1])).astype(BF16)
            part = _dot(act, wd_ref[c0:c1, :])
            f[i] = part if f[i] is None else f[i] + part
    for r, x, fi in zip(subs, x1, f):
        y_ref[r, :] = x + _rms(fi, nof_ref[...])


def _out_ffn(x2d, ys, o, gs, ga, wo, npm, npf, wg, wu, wd, nof, tm):
    m = x2d.shape[0]
    row = pl.BlockSpec((tm, D_MODEL), lambda i: (i, 0))
    vec = _resident((1, D_MODEL))
    return pl.pallas_call(
        _out_ffn_kernel, grid=(m // tm,),
        in_specs=[row, row, row, row, row, _resident((D_MODEL, D_MODEL)), vec, vec,
                  _resident((D_MODEL, D_FF)), _resident((D_MODEL, D_FF)), _resident((D_FF, D_MODEL)), vec],
        out_specs=row, out_shape=jax.ShapeDtypeStruct((m, D_MODEL), F32),
        compiler_params=_params(("parallel",)), name="out_ffn")(x2d, ys, o, gs, ga, wo, npm, npf, wg, wu, wd, nof)


def kernel(x_prompt, x_sample, cache_k, cache_v, state_ssm_re, state_ssm_im, page_table, norm_pre_mix, w_in,
           ssm_lambda_re, ssm_lambda_im, ssm_log_dt, ssm_b_re, ssm_b_im, ssm_c_re, ssm_c_im, ssm_d, w_glu_a,
           w_glu_b, lambda_q1, lambda_k1, lambda_q2, lambda_k2, subln_gain, w_o, norm_post_mix, norm_pre_ffn,
           w_gate, w_up, w_down, norm_post_ffn):
    assert w_in.shape[0] == 1, "single-layer stack"
    bp, tp, _ = x_prompt.shape
    bs, ts, _ = x_sample.shape
    assert ts == 1

    are, aim, bbre, bbim, lam = _prep(ssm_lambda_re[0], ssm_lambda_im[0], ssm_log_dt[0], ssm_b_re[0], ssm_b_im[0],
                                      lambda_q1[0], lambda_k1[0], lambda_q2[0], lambda_k2[0])
    are, aim = are.reshape(1, SSM_LANES), aim.reshape(1, SSM_LANES)
    bd = jnp.concatenate([_block_diag(bbre), _block_diag(bbim)], axis=1).astype(BF16)
    cre = _block_diag(jnp.swapaxes(ssm_c_re[0], 1, 2)).astype(BF16)
    cim = _block_diag(jnp.swapaxes(-ssm_c_im[0], 1, 2)).astype(BF16)
    dsk = ssm_d[0].reshape(1, SSM_WIDTH)
    wglu = jnp.concatenate([w_glu_a[0], w_glu_b[0]], axis=1).astype(BF16)
    vec = lambda a: a[0].reshape(1, -1)
    w_in_bf, wo_bf = w_in[0].astype(BF16), w_o[0].astype(BF16)
    wg_bf, wu_bf, wd_bf = w_gate[0].astype(BF16), w_up[0].astype(BF16), w_down[0].astype(BF16)
    gain = vec(subln_gain)
    ssm_w = (are, aim, bd, cre, cim, dsk, wglu)
    ffn_w = (wo_bf, vec(norm_post_mix), vec(norm_pre_ffn), wg_bf, wu_bf, wd_bf, vec(norm_post_ffn))

    xp = x_prompt.reshape(bp * tp, D_MODEL)
    u_il, q, kf, kb, vf, vb, gs, ga = _in_proj(xp, np.arange(tp), bp, vec(norm_pre_mix), w_in_bf, tm=PROJ_ROWS)
    ys, hpr, hpi = _ssm_prompt(u_il, bp, tp, *ssm_w, lc=SSM_CHUNK)

    xs = x_sample.reshape(bs, D_MODEL)
    pos_s = np.full((bs,), PAST_LEN)
    u_s, q_s, kf_s, kb_s, vf_s, vb_s, gs_s, ga_s = _in_proj(xs, pos_s, 1, vec(norm_pre_mix), w_in_bf, tm=bs)
    ys_s, hsr, hsi = _ssm_step(u_s, state_ssm_re[0].reshape(bs, SSM_LANES), state_ssm_im[0].reshape(bs, SSM_LANES),
                               *ssm_w)

    n_phys = cache_k.shape[1]
    pages = lambda c: c[0].reshape(n_phys, PAGE_SIZE * N_HEADS, QK_DIM)
    r3 = lambda a: a.reshape(bp, tp, ATTN_WIDTH)
    o, o_s = _attention(page_table, lam, r3(q), r3(kb), r3(vb), gain, q_s, pages(cache_k), pages(cache_v),
                        kb_s, vb_s, bp, tp, tq=ATTN_Q_ROWS)

    yp = _out_ffn(xp, ys.reshape(bp * tp, D_MODEL), o.reshape(bp * tp, ATTN_WIDTH), gs, ga, *ffn_w, tm=PROJ_ROWS)
    ysamp = _out_ffn(xs, ys_s, o_s.reshape(bs, ATTN_WIDTH), gs_s, ga_s, *ffn_w, tm=bs)

    st = lambda a, n: a.reshape(1, n, SSM_GROUPS, SSM_STATE)
    kv = lambda a, n, t: a.reshape(1, n, t, N_HEADS, QK_DIM)
    return (yp.reshape(bp, tp, D_MODEL), ysamp.reshape(bs, ts, D_MODEL),
            kv(kf, bp, tp), kv(vf, bp, tp), st(hpr, bp), st(hpi, bp),
            kv(kf_s, bs, ts), kv(vf_s, bs, ts), st(hsr, bs), st(hsi, bs))
```

```python
import functools
import math

import jax
import jax.numpy as jnp
import numpy as np
from jax import lax
from jax.experimental import pallas as pl
from jax.experimental.pallas import tpu as pltpu

F32 = jnp.float32
BF16 = jnp.bfloat16

D_MODEL = 1024
PAST_LEN = 2048
PAGE_SIZE = 128
SSM_WIDTH = 512
SSM_GROUP = 16
SSM_GROUPS = 32
SSM_STATE = 64
SSM_LANES = SSM_GROUPS * SSM_STATE
HEAD_DIM = 64
N_HEADS = 8
QK_DIM = 128
ATTN_WIDTH = 1024
ROPE_DIM = 16
ROPE_THETA = 500000.0
D_FF = 2816
IN_WIDTH = SSM_WIDTH + 3 * ATTN_WIDTH + 2 * D_MODEL
EPS = 1e-6
LAM_INIT = 0.8 - 0.6 * math.exp(-0.3 * 0)
Q_SCALE = HEAD_DIM ** -0.5 * math.log2(math.e)

LANES = 128
SUBLANES = 8
MXU_WIDTH = 256
VMEM_LIMIT = 56 * 1024 * 1024
ATTN_VMEM_LIMIT = 62 * 1024 * 1024

PROJ_ROWS = 512
SSM_CHUNK = 128
ATTN_Q_ROWS = 1024
DECODE_GROUPS = 2
ROW_SUBTILES = 2
MIN_SUBTILE_ROWS = MXU_WIDTH


def _params(sem, vmem=VMEM_LIMIT):
    return pltpu.CompilerParams(dimension_semantics=sem, vmem_limit_bytes=vmem)


def _row_subtiles(tm):
    sub = max(tm // ROW_SUBTILES, min(tm, MIN_SUBTILE_ROWS))
    return [slice(r0, r0 + sub) for r0 in range(0, tm, sub)]


def _resident(shape):
    return pl.BlockSpec(shape, lambda *_: (0,) * len(shape), pipeline_mode=pl.Buffered(1))


def _rms(x, g):
    return x * lax.rsqrt(jnp.mean(x * x, axis=-1, keepdims=True) + EPS) * g


def _dot(a, b):
    return jnp.dot(a, b, preferred_element_type=F32)


def _dot_nt(a, b):
    return lax.dot_general(a, b, (((1,), (1,)), ((), ())), preferred_element_type=F32)


def _prep_kernel(lre_ref, lim_ref, ldt_ref, bre_ref, bim_ref, lq1_ref, lk1_ref, lq2_ref, lk2_ref,
                 are_ref, aim_ref, bbre_ref, bbim_ref, lam_ref):
    lre, lim = lre_ref[...], lim_ref[...]
    dt = jnp.exp(ldt_ref[...])
    mag = jnp.exp(lre * dt)
    ar, ai = mag * jnp.cos(lim * dt), mag * jnp.sin(lim * dt)
    den = lre * lre + lim * lim
    fr = ((ar - 1.0) * lre + ai * lim) / den
    fi = (ai * lre - (ar - 1.0) * lim) / den
    are_ref[...] = ar
    aim_ref[...] = ai
    bre, bim = bre_ref[...], bim_ref[...]
    bbre_ref[...] = fr[:, None, :] * bre - fi[:, None, :] * bim
    bbim_ref[...] = fr[:, None, :] * bim + fi[:, None, :] * bre
    s1 = jnp.sum(lq1_ref[...] * lk1_ref[...], axis=-1, keepdims=True)
    s2 = jnp.sum(lq2_ref[...] * lk2_ref[...], axis=-1, keepdims=True)
    lam_ref[...] = jnp.exp(s1) - jnp.exp(s2) + LAM_INIT


def _prep(lam_re, lam_im, log_dt, b_re, b_im, lq1, lk1, lq2, lk2):
    g, p, c = SSM_GROUPS, SSM_STATE, SSM_GROUP
    out_shape = (jax.ShapeDtypeStruct((g, p), F32), jax.ShapeDtypeStruct((g, p), F32),
                 jax.ShapeDtypeStruct((g, c, p), F32), jax.ShapeDtypeStruct((g, c, p), F32),
                 jax.ShapeDtypeStruct((1, 1), F32))
    return pl.pallas_call(_prep_kernel, out_shape=out_shape, name="ssm_prep")(
        lam_re, lam_im, log_dt.reshape(g, 1), jnp.swapaxes(b_re, 1, 2), jnp.swapaxes(b_im, 1, 2),
        lq1.reshape(1, HEAD_DIM), lk1.reshape(1, HEAD_DIM), lq2.reshape(1, HEAD_DIM), lk2.reshape(1, HEAD_DIM))


def _block_diag(blocks):
    g, r, c = blocks.shape
    keep = np.arange(g * r)[:, None] // r == np.arange(g * c)[None, :] // c
    return jnp.where(keep, jnp.tile(blocks.reshape(g * r, c), (1, g)), 0.0)


def _in_proj_kernel(x_ref, g_ref, w_ref, cos_ref, sa_ref, sb_ref,
                    u_ref, q_ref, kf_ref, kb_ref, vf_ref, vb_ref, gs_ref, ga_ref):
    o0 = SSM_WIDTH
    o1, o2, o3, o4 = o0 + ATTN_WIDTH, o0 + 2 * ATTN_WIDTH, o0 + 3 * ATTN_WIDTH, o0 + 3 * ATTN_WIDTH + D_MODEL
    subs = _row_subtiles(x_ref.shape[0])
    sub = subs[0].stop
    heads = [slice(hh * QK_DIM, (hh + 1) * QK_DIM) for hh in range(N_HEADS)]

    def rope(z, rows):
        return (z * cos_ref[rows, :] + pltpu.roll(z, LANES - ROPE_DIM // 2, 1) * sa_ref[rows, :]
                + pltpu.roll(z, ROPE_DIM // 2, 1) * sb_ref[rows, :])

    def seg_u(rows, h):
        u_ref[rows, :] = _dot(h, w_ref[:, 0:o0]).astype(BF16)

    def seg_q(rows, h):
        zq = _dot(h, w_ref[:, o0:o1])
        for sl in heads:
            q_ref[rows, sl] = (rope(zq[:, sl], rows) * Q_SCALE).astype(BF16)

    def seg_k(rows, h):
        zk = _dot(h, w_ref[:, o1:o2])
        for hh, sl in enumerate(heads):
            kr = rope(zk[:, sl], rows)
            kf_ref[pl.ds(rows.start * N_HEADS + hh, sub, stride=N_HEADS), :] = kr
            kb_ref[rows, sl] = kr.astype(BF16)

    def seg_v(rows, h):
        zv = _dot(h, w_ref[:, o2:o3])
        for hh, sl in enumerate(heads):
            vf_ref[pl.ds(rows.start * N_HEADS + hh, sub, stride=N_HEADS), :] = zv[:, sl]
        vb_ref[rows, :] = zv.astype(BF16)

    def seg_gs(rows, h):
        gs_ref[rows, :] = jax.nn.sigmoid(_dot(h, w_ref[:, o3:o4])).astype(BF16)

    def seg_ga(rows, h):
        ga_ref[rows, :] = jax.nn.sigmoid(_dot(h, w_ref[:, o4:IN_WIDTH])).astype(BF16)

    hs = [_rms(x_ref[rows, :], g_ref[...]).astype(BF16) for rows in subs]
    for seg in (seg_u, seg_q, seg_k, seg_v, seg_gs, seg_ga):
        for rows, h in zip(subs, hs):
            seg(rows, h)


def _rope_tables(pos):
    half = ROPE_DIM // 2
    inv = ROPE_THETA ** (-np.arange(half, dtype=np.float64) * 2.0 / ROPE_DIM)
    ang = np.asarray(pos, np.float64)[:, None] * inv[None, :]
    cos, sin = np.cos(ang), np.sin(ang)
    n = ang.shape[0]
    comp_cos = np.concatenate([cos, cos, np.ones((n, HEAD_DIM - ROPE_DIM))], axis=1)
    comp_sa = np.concatenate([-sin, np.zeros((n, HEAD_DIM - half))], axis=1)
    comp_sb = np.concatenate([np.zeros((n, half)), sin, np.zeros((n, HEAD_DIM - ROPE_DIM))], axis=1)
    two = lambda t: jnp.asarray(np.concatenate([t, t], axis=1), F32)
    return two(comp_cos), two(comp_sa), two(comp_sb)


def _in_proj(x2d, pos, nb, g_pre, w_in_bf, tm):
    m = x2d.shape[0]
    t_len = m // nb
    nt = t_len // tm
    cos, sa, sb = _rope_tables(pos)
    row = lambda w: pl.BlockSpec((tm, w), lambda i: (i, 0))
    heads = pl.BlockSpec((tm * N_HEADS, QK_DIM), lambda i: (i, 0))
    tab = pl.BlockSpec((tm, LANES), lambda i: (i % nt, 0))
    out_shape = (jax.ShapeDtypeStruct((t_len, nb * SSM_WIDTH), BF16),
                 jax.ShapeDtypeStruct((m, ATTN_WIDTH), BF16),
                 jax.ShapeDtypeStruct((m * N_HEADS, QK_DIM), F32), jax.ShapeDtypeStruct((m, ATTN_WIDTH), BF16),
                 jax.ShapeDtypeStruct((m * N_HEADS, QK_DIM), F32), jax.ShapeDtypeStruct((m, ATTN_WIDTH), BF16),
                 jax.ShapeDtypeStruct((m, D_MODEL), BF16), jax.ShapeDtypeStruct((m, D_MODEL), BF16))
    out_specs = (pl.BlockSpec((tm, SSM_WIDTH), lambda i: (i % nt, i // nt)),
                 row(ATTN_WIDTH), heads, row(ATTN_WIDTH), heads, row(ATTN_WIDTH),
                 row(D_MODEL), row(D_MODEL))
    return pl.pallas_call(
        _in_proj_kernel, grid=(m // tm,),
        in_specs=[row(D_MODEL), _resident((1, D_MODEL)), _resident((D_MODEL, IN_WIDTH)), tab, tab, tab],
        out_specs=out_specs, out_shape=out_shape,
        compiler_params=_params(("parallel",)), name="in_proj")(x2d, g_pre, w_in_bf, cos, sa, sb)


def _ssm_drive(u, bd_ref, store):
    n_tiles = 2 * SSM_LANES // MXU_WIDTH
    for j in range(n_tiles):
        ks = LANES * ((j % (n_tiles // 2)) // 2)
        lanes = slice(MXU_WIDTH * j, MXU_WIDTH * (j + 1))
        store(lanes, _dot(u[:, ks:ks + LANES], bd_ref[ks:ks + LANES, lanes]))


def _ssm_readout(load, u, cre_ref, cim_ref, dsk_ref):
    half = SSM_LANES // 2
    ys = []
    for j in range(2):
        hre = load(slice(half * j, half * (j + 1))).astype(BF16)
        him = load(slice(SSM_LANES + half * j, SSM_LANES + half * (j + 1))).astype(BF16)
        cols = slice(MXU_WIDTH * j, MXU_WIDTH * (j + 1))
        ys.append(_dot(hre, cre_ref[half * j:half * (j + 1), cols]) + _dot(him, cim_ref[half * j:half * (j + 1), cols]))
    y = jnp.concatenate(ys, axis=1) + dsk_ref[...] * u.astype(F32)
    return jax.nn.gelu(y)


def _glu(yg, wglu_ref):
    return _dot(yg, wglu_ref[:, 0:D_MODEL]) * jax.nn.sigmoid(_dot(yg, wglu_ref[:, D_MODEL:2 * D_MODEL]))


def _ssm_prompt_kernel(u_ref, are_ref, aim_ref, bd_ref, cre_ref, cim_ref, dsk_ref, wglu_ref,
                       ys_ref, hre_ref, him_ref, h_s, il_s, st_re, st_im, *, nb, lc):
    @pl.when(pl.program_id(0) == 0)
    def _():
        st_re[...] = jnp.zeros_like(st_re)
        st_im[...] = jnp.zeros_like(st_im)

    rows = lc * nb
    n_tr = rows // SUBLANES
    n_slabs = SSM_WIDTH // LANES
    slab = lambda j: slice(j * LANES, (j + 1) * LANES)
    seq_rows = lambda b: pl.ds(b, lc, stride=nb)
    for b in range(nb):
        ub = u_ref[:, b * SSM_WIDTH:(b + 1) * SSM_WIDTH].astype(F32)
        for j in range(n_slabs):
            il_s[j, seq_rows(b), :] = ub[:, slab(j)]
    u = jnp.concatenate([il_s[j] for j in range(n_slabs)], axis=1).astype(BF16)

    def store(lanes, val):
        h_s[:, :, lanes] = val.reshape(n_tr, SUBLANES, val.shape[-1])

    _ssm_drive(u, bd_ref, store)

    ar, ai = are_ref[...], aim_ref[...]
    hr, hi = st_re[...], st_im[...]
    for s in range(n_tr):
        for k in range(SUBLANES // nb):
            sub = slice(k * nb, (k + 1) * nb)
            nhr = ar * hr - ai * hi + h_s[s, sub, 0:SSM_LANES]
            nhi = ar * hi + ai * hr + h_s[s, sub, SSM_LANES:2 * SSM_LANES]
            h_s[s, sub, 0:SSM_LANES] = nhr
            h_s[s, sub, SSM_LANES:2 * SSM_LANES] = nhi
            hr, hi = nhr, nhi
    st_re[...] = hr
    st_im[...] = hi
    hre_ref[...] = hr
    him_ref[...] = hi

    load = lambda lanes: h_s[:, :, lanes].reshape(rows, lanes.stop - lanes.start)
    yg = _ssm_readout(load, u, cre_ref, cim_ref, dsk_ref)
    for j in range(n_slabs):
        il_s[j] = yg[:, slab(j)]
    per_seq = [jnp.concatenate([il_s[j, seq_rows(b), :] for j in range(n_slabs)], axis=1) for b in range(nb)]
    res = _glu(jnp.concatenate(per_seq, axis=0).astype(BF16), wglu_ref)
    for b in range(nb):
        ys_ref[b] = res[b * lc:(b + 1) * lc].astype(BF16)


def _ssm_prompt(u_il, nb, t_len, are, aim, bd, cre, cim, dsk, wglu, lc):
    rows = lc * nb
    kern = functools.partial(_ssm_prompt_kernel, nb=nb, lc=lc)
    out_shape = (jax.ShapeDtypeStruct((nb, t_len, D_MODEL), BF16),
                 jax.ShapeDtypeStruct((nb, SSM_LANES), F32), jax.ShapeDtypeStruct((nb, SSM_LANES), F32))
    return pl.pallas_call(
        kern, grid=(t_len // lc,),
        in_specs=[pl.BlockSpec((lc, nb * SSM_WIDTH), lambda c: (c, 0)),
                  _resident((nb, SSM_LANES)), _resident((nb, SSM_LANES)),
                  _resident((SSM_WIDTH, 2 * SSM_LANES)), _resident((SSM_LANES, SSM_WIDTH)),
                  _resident((SSM_LANES, SSM_WIDTH)), _resident((1, SSM_WIDTH)),
                  _resident((SSM_WIDTH, 2 * D_MODEL))],
        out_specs=(pl.BlockSpec((nb, lc, D_MODEL), lambda c: (0, c, 0)),
                   pl.BlockSpec((nb, SSM_LANES), lambda c: (0, 0)), pl.BlockSpec((nb, SSM_LANES), lambda c: (0, 0))),
        out_shape=out_shape,
        scratch_shapes=[pltpu.VMEM((rows // SUBLANES, SUBLANES, 2 * SSM_LANES), F32),
                        pltpu.VMEM((SSM_WIDTH // LANES, rows, LANES), F32),
                        pltpu.VMEM((nb, SSM_LANES), F32), pltpu.VMEM((nb, SSM_LANES), F32)],
        compiler_params=_params(("arbitrary",)), name="ssm_prompt")(
            u_il, jnp.tile(are, (nb, 1)), jnp.tile(aim, (nb, 1)), bd, cre, cim, dsk, wglu)


def _ssm_step_kernel(u_ref, h0re_ref, h0im_ref, are_ref, aim_ref, bd_ref, cre_ref, cim_ref, dsk_ref, wglu_ref,
                     ys_ref, hre_ref, him_ref, h_s):
    u = u_ref[...]

    def store(lanes, val):
        h_s[:, lanes] = val

    _ssm_drive(u, bd_ref, store)
    ar, ai = are_ref[...], aim_ref[...]
    hr, hi = h0re_ref[...], h0im_ref[...]
    nhr = ar * hr - ai * hi + h_s[:, 0:SSM_LANES]
    nhi = ar * hi + ai * hr + h_s[:, SSM_LANES:2 * SSM_LANES]
    hre_ref[...] = nhr
    him_ref[...] = nhi
    h_s[:, 0:SSM_LANES] = nhr
    h_s[:, SSM_LANES:2 * SSM_LANES] = nhi
    yg = _ssm_readout(lambda lanes: h_s[:, lanes], u, cre_ref, cim_ref, dsk_ref)
    ys_ref[...] = _glu(yg.astype(BF16), wglu_ref).astype(BF16)


def _ssm_step(u, h0re, h0im, are, aim, bd, cre, cim, dsk, wglu):
    n = u.shape[0]
    out_shape = (jax.ShapeDtypeStruct((n, D_MODEL), BF16),
                 jax.ShapeDtypeStruct((n, SSM_LANES), F32), jax.ShapeDtypeStruct((n, SSM_LANES), F32))
    return pl.pallas_call(
        _ssm_step_kernel, out_shape=out_shape,
        scratch_shapes=[pltpu.VMEM((n, 2 * SSM_LANES), F32)],
        compiler_params=pltpu.CompilerParams(vmem_limit_bytes=VMEM_LIMIT), name="ssm_step")(
            u, h0re, h0im, are, aim, bd, cre, cim, dsk, wglu)


def _subln(o1, l1, o2, l2, lam, gain):
    o = o1 / l1 - lam * (o2 / l2)
    return _rms(o, gain) * (1.0 - LAM_INIT)


def _prompt_tile(qi, lam_ref, q_ref, k_ref, v_ref, gain_ref, o_ref, sa_s, sb_s, m_s, acc_s, tq, tk):
    lane = lax.broadcasted_iota(jnp.int32, (tk, QK_DIM), 1)
    parts = []
    for half in range(2):
        q = q_ref[half * tk:(half + 1) * tk, :]
        zero = jnp.zeros_like(q)
        parts += [jnp.where(lane < HEAD_DIM, q, zero), jnp.where(lane >= HEAD_DIM, q, zero)]
    qz = jnp.concatenate(parts, axis=0)
    ones = jnp.ones((tk, LANES), BF16)
    wide = lambda a: jnp.concatenate([a] * (tk // LANES), axis=1)

    def scores(c, s_ref, rows=slice(0, 2 * tq)):
        s_ref[rows, :] = _dot_nt(qz[rows], k_ref[c * tk:(c + 1) * tk, :])

    def softmax_pv(c, s_ref, rows=slice(0, 2 * tq), diag_rows=0, first=False):
        s = s_ref[rows, :]
        if diag_rows:
            row = lax.broadcasted_iota(jnp.int32, (diag_rows, tk), 0) % tk
            col = lax.broadcasted_iota(jnp.int32, (diag_rows, tk), 1)
            masked = jnp.where(col <= row, s[0:diag_rows], -jnp.inf)
            s = masked if diag_rows == s.shape[0] else jnp.concatenate([masked, s[diag_rows:]], axis=0)
        m_cur = jnp.max(s, axis=1, keepdims=True)
        v1 = jnp.concatenate([v_ref[c * tk:(c + 1) * tk, :], ones], axis=1)
        if first:
            m_new = jnp.broadcast_to(m_cur, (s.shape[0], LANES))
            acc_s[rows, :] = _dot(jnp.exp2(s - wide(m_new)).astype(BF16), v1)
        else:
            m_prev = m_s[rows, :]
            m_new = jnp.maximum(m_prev, m_cur)
            alpha = jnp.exp2(m_prev - m_new)
            p = jnp.exp2(s - wide(m_new)).astype(BF16)
            acc_s[rows, :] = jnp.concatenate([alpha, alpha], axis=1) * acc_s[rows, :] + _dot(p, v1)
        m_s[rows, :] = m_new

    scores(0, sa_s)
    for j in range(qi):
        c0 = 2 * j
        scores(c0 + 1, sb_s)
        softmax_pv(c0, sa_s, first=j == 0)
        scores(c0 + 2, sa_s)
        softmax_pv(c0 + 1, sb_s)
    c0 = 2 * qi
    late = slice(tq, 2 * tq)
    scores(c0 + 1, sb_s, late)
    softmax_pv(c0, sa_s, diag_rows=tq, first=qi == 0)
    softmax_pv(c0 + 1, sb_s, late, diag_rows=tq)
    acc = acc_s[...]
    lam, gain = lam_ref[0, 0], gain_ref[...]
    for half in range(2):
        a1 = acc[2 * half * tk:(2 * half + 1) * tk]
        a2 = acc[(2 * half + 1) * tk:(2 * half + 2) * tk]
        o_ref[half * tk:(half + 1) * tk, :] = _subln(a1[:, 0:QK_DIM], a1[:, QK_DIM:2 * QK_DIM], a2[:, 0:QK_DIM],
                                                     a2[:, QK_DIM:2 * QK_DIM], lam, gain).astype(BF16)


def _decode_query(q_ref):
    q8 = q_ref[0].astype(F32)
    lane = lax.broadcasted_iota(jnp.int32, (N_HEADS, QK_DIM), 1)
    qz32 = jnp.concatenate([jnp.where(lane < HEAD_DIM, q8, 0.0), jnp.where(lane >= HEAD_DIM, q8, 0.0)], axis=0)
    return qz32, qz32.astype(BF16)


def _decode_groups(q_ref, k_groups, v_groups):
    n_maps = 2 * N_HEADS
    rows_pp = PAGE_SIZE * N_HEADS
    _, qz = _decode_query(q_ref)
    scores = [jnp.concatenate([_dot_nt(qz, r[...].astype(BF16)) for r in ks], axis=1) for ks in k_groups]
    stats, probs = [], []
    for s in scores:
        col = lax.broadcasted_iota(jnp.int32, s.shape, 1)
        row = lax.broadcasted_iota(jnp.int32, s.shape, 0)
        s = jnp.where(col % N_HEADS == row % N_HEADS, s, -jnp.inf)
        m = jnp.max(s, axis=1, keepdims=True)
        p = jnp.exp2(s - m)
        stats.append((m, jnp.sum(p, axis=1, keepdims=True)))
        probs.append(p.astype(BF16))
    out = []
    for (m, l), pb, vs in zip(stats, probs, v_groups):
        pv = _dot(pb[:, 0:rows_pp], vs[0][...].astype(BF16))
        for i in range(1, len(vs)):
            pv = pv + _dot(pb[:, i * rows_pp:(i + 1) * rows_pp], vs[i][...].astype(BF16))
        out.append((m, l, pv))
    return out


def _decode_finish(groups, lam_ref, gain_ref, q_ref, kn_ref, vn_ref, o_ref):
    n_maps = 2 * N_HEADS
    qz32, _ = _decode_query(q_ref)
    two = lambda a: jnp.concatenate([a, a], axis=0)
    s_n = jnp.sum(qz32 * two(kn_ref[0].astype(F32)), axis=1, keepdims=True)
    m = s_n
    for mg, _, _ in groups:
        m = jnp.maximum(m, mg)
    l = jnp.exp2(s_n - m)
    acc = l * two(vn_ref[0].astype(F32))
    for mg, lg, pvg in groups:
        w = jnp.exp2(mg - m)
        l = l + w * lg
        acc = acc + w * pvg
    o_ref[0] = _subln(acc[0:N_HEADS], l[0:N_HEADS], acc[N_HEADS:n_maps], l[N_HEADS:n_maps], lam_ref[0, 0],
                      gain_ref[...]).astype(BF16)


def _attn_kernel(pt_ref, lam_ref, q_ref, k_ref, v_ref, gain_ref, qs_ref, *refs, tq, tk, n_pages):
    del pt_ref
    k_refs, v_refs = refs[0:n_pages], refs[n_pages:2 * n_pages]
    kn_ref, vn_ref, o_ref, os_ref, sa_s, sb_s, m_s, acc_s = refs[2 * n_pages:]
    per = n_pages // DECODE_GROUPS
    split = lambda page_refs: [page_refs[g * per:(g + 1) * per] for g in range(DECODE_GROUPS)]
    for qi in range(k_ref.shape[0] // tq):
        @pl.when(pl.program_id(2) == qi)
        def _(qi=qi):
            _prompt_tile(qi, lam_ref, q_ref, k_ref, v_ref, gain_ref, o_ref, sa_s, sb_s, m_s, acc_s, tq, tk)
            groups = _decode_groups(qs_ref, split(k_refs), split(v_refs))
            _decode_finish(groups, lam_ref, gain_ref, qs_ref, kn_ref, vn_ref, os_ref)


def _attention(page_table, lam, q, k, v, gain, q_s, cache_k, cache_v, k_new, v_new, nb, t_len, tq):
    tk = tq // 2
    n, n_pages = page_table.shape
    nq = t_len // tq
    assert nb * N_HEADS * nq == n, "one decode sample per prompt grid step"
    kern = functools.partial(_attn_kernel, tq=tq, tk=tk, n_pages=n_pages)
    sample = lambda b, h, i: (b * N_HEADS + h) * nq + i
    qspec = pl.BlockSpec((None, tq, QK_DIM), lambda b, h, i, pt: (b, i, h))
    kvspec = pl.BlockSpec((None, t_len, QK_DIM), lambda b, h, i, pt: (b, 0, h))
    row = pl.BlockSpec((1, N_HEADS, QK_DIM), lambda b, h, i, pt: (sample(b, h, i), 0, 0))

    def page(j):
        return pl.BlockSpec((None, PAGE_SIZE * N_HEADS, QK_DIM), lambda b, h, i, pt: (pt[sample(b, h, i), j], 0, 0))

    pages = [page(j) for j in range(n_pages)]
    grid_spec = pltpu.PrefetchScalarGridSpec(
        num_scalar_prefetch=1, grid=(nb, N_HEADS, nq),
        in_specs=[pl.BlockSpec(memory_space=pltpu.SMEM), qspec, kvspec, kvspec,
                  pl.BlockSpec((1, QK_DIM), lambda b, h, i, pt: (0, 0)), row] + pages + pages + [row, row],
        out_specs=(qspec, row),
        scratch_shapes=[pltpu.VMEM((2 * tq, tk), F32), pltpu.VMEM((2 * tq, tk), F32),
                        pltpu.VMEM((2 * tq, LANES), F32), pltpu.VMEM((2 * tq, 2 * QK_DIM), F32)])
    r3 = lambda a: a.reshape(n, N_HEADS, QK_DIM)
    out_shape = (jax.ShapeDtypeStruct((nb, t_len, ATTN_WIDTH), BF16), jax.ShapeDtypeStruct((n, N_HEADS, QK_DIM), BF16))
    return pl.pallas_call(
        kern, grid_spec=grid_spec, out_shape=out_shape,
        compiler_params=_params(("arbitrary", "arbitrary", "arbitrary"), ATTN_VMEM_LIMIT), name="attention")(
            page_table, lam, q, k, v, gain, r3(q_s), *([cache_k] * n_pages), *([cache_v] * n_pages),
            r3(k_new), r3(v_new))


FF_CHUNKS = ((0, 1024), (1024, 2048), (2048, D_FF))


def _out_ffn_kernel(x_ref, ys_ref, o_ref, gs_ref, ga_ref, wo_ref, npm_ref, npf_ref, wg_ref, wu_ref, wd_ref, nof_ref,
                    y_ref):
    subs = _row_subtiles(x_ref.shape[0])
    f32 = lambda ref, rows: ref[rows, :].astype(F32)
    mixed = [(f32(gs_ref, r) * f32(ys_ref, r) + f32(ga_ref, r) * f32(o_ref, r)).astype(BF16) for r in subs]
    proj = [_dot(m, wo_ref[...]) for m in mixed]
    x1 = [x_ref[r, :] + _rms(p, npm_ref[...]) for r, p in zip(subs, proj)]
    hf = [_rms(x, npf_ref[...]).astype(BF16) for x in x1]
    f = [None] * len(subs)
    for c0, c1 in FF_CHUNKS:
        for i, h in enumerate(hf):
            act = (jax.nn.silu(_dot(h, wg_ref[:, c0:c1])) * _dot(h, wu_ref[:, c0:c1])).astype(BF16)
            part = _dot(act, wd_ref[c0:c1, :])
            f[i] = part if f[i] is None else f[i] + part
    for r, x, fi in zip(subs, x1, f):
        y_ref[r, :] = x + _rms(fi, nof_ref[...])


def _out_ffn(x2d, ys, o, gs, ga, wo, npm, npf, wg, wu, wd, nof, tm):
    m = x2d.shape[0]
    row = pl.BlockSpec((tm, D_MODEL), lambda i: (i, 0))
    vec = _resident((1, D_MODEL))
    return pl.pallas_call(
        _out_ffn_kernel, grid=(m // tm,),
        in_specs=[row, row, row, row, row, _resident((D_MODEL, D_MODEL)), vec, vec,
                  _resident((D_MODEL, D_FF)), _resident((D_MODEL, D_FF)), _resident((D_FF, D_MODEL)), vec],
        out_specs=row, out_shape=jax.ShapeDtypeStruct((m, D_MODEL), F32),
        compiler_params=_params(("parallel",)), name="out_ffn")(x2d, ys, o, gs, ga, wo, npm, npf, wg, wu, wd, nof)


def kernel(x_prompt, x_sample, cache_k, cache_v, state_ssm_re, state_ssm_im, page_table, norm_pre_mix, w_in,
           ssm_lambda_re, ssm_lambda_im, ssm_log_dt, ssm_b_re, ssm_b_im, ssm_c_re, ssm_c_im, ssm_d, w_glu_a,
           w_glu_b, lambda_q1, lambda_k1, lambda_q2, lambda_k2, subln_gain, w_o, norm_post_mix, norm_pre_ffn,
           w_gate, w_up, w_down, norm_post_ffn):
    assert w_in.shape[0] == 1, "single-layer stack"
    bp, tp, _ = x_prompt.shape
    bs, ts, _ = x_sample.shape
    assert ts == 1

    are, aim, bbre, bbim, lam = _prep(ssm_lambda_re[0], ssm_lambda_im[0], ssm_log_dt[0], ssm_b_re[0], ssm_b_im[0],
                                      lambda_q1[0], lambda_k1[0], lambda_q2[0], lambda_k2[0])
    are, aim = are.reshape(1, SSM_LANES), aim.reshape(1, SSM_LANES)
    bd = jnp.concatenate([_block_diag(bbre), _block_diag(bbim)], axis=1).astype(BF16)
    cre = _block_diag(jnp.swapaxes(ssm_c_re[0], 1, 2)).astype(BF16)
    cim = _block_diag(jnp.swapaxes(-ssm_c_im[0], 1, 2)).astype(BF16)
    dsk = ssm_d[0].reshape(1, SSM_WIDTH)
    wglu = jnp.concatenate([w_glu_a[0], w_glu_b[0]], axis=1).astype(BF16)
    vec = lambda a: a[0].reshape(1, -1)
    w_in_bf, wo_bf = w_in[0].astype(BF16), w_o[0].astype(BF16)
    wg_bf, wu_bf, wd_bf = w_gate[0].astype(BF16), w_up[0].astype(BF16), w_down[0].astype(BF16)
    gain = vec(subln_gain)
    ssm_w = (are, aim, bd, cre, cim, dsk, wglu)
    ffn_w = (wo_bf, vec(norm_post_mix), vec(norm_pre_ffn), wg_bf, wu_bf, wd_bf, vec(norm_post_ffn))

    xp = x_prompt.reshape(bp * tp, D_MODEL)
    u_il, q, kf, kb, vf, vb, gs, ga = _in_proj(xp, np.arange(tp), bp, vec(norm_pre_mix), w_in_bf, tm=PROJ_ROWS)
    ys, hpr, hpi = _ssm_prompt(u_il, bp, tp, *ssm_w, lc=SSM_CHUNK)

    xs = x_sample.reshape(bs, D_MODEL)
    pos_s = np.full((bs,), PAST_LEN)
    u_s, q_s, kf_s, kb_s, vf_s, vb_s, gs_s, ga_s = _in_proj(xs, pos_s, 1, vec(norm_pre_mix), w_in_bf, tm=bs)
    ys_s, hsr, hsi = _ssm_step(u_s, state_ssm_re[0].reshape(bs, SSM_LANES), state_ssm_im[0].reshape(bs, SSM_LANES),
                               *ssm_w)

    n_phys = cache_k.shape[1]
    pages = lambda c: c[0].reshape(n_phys, PAGE_SIZE * N_HEADS, QK_DIM)
    r3 = lambda a: a.reshape(bp, tp, ATTN_WIDTH)
    o, o_s = _attention(page_table, lam, r3(q), r3(kb), r3(vb), gain, q_s, pages(cache_k), pages(cache_v),
                        kb_s, vb_s, bp, tp, tq=ATTN_Q_ROWS)

    yp = _out_ffn(xp, ys.reshape(bp * tp, D_MODEL), o.reshape(bp * tp, ATTN_WIDTH), gs, ga, *ffn_w, tm=PROJ_ROWS)
    ysamp = _out_ffn(xs, ys_s, o_s.reshape(bs, ATTN_WIDTH), gs_s, ga_s, *ffn_w, tm=bs)

    st = lambda a, n: a.reshape(1, n, SSM_GROUPS, SSM_STATE)
    kv = lambda a, n, t: a.reshape(1, n, t, N_HEADS, QK_DIM)
    return (yp.reshape(bp, tp, D_MODEL), ysamp.reshape(bs, ts, D_MODEL),
            kv(kf, bp, tp), kv(vf, bp, tp), st(hpr, bp), st(hpi, bp),
            kv(kf_s, bs, ts), kv(vf_s, bs, ts), st(hsr, bs), st(hsi, bs))
```

```python
import functools
import math

import jax
import jax.numpy as jnp
import numpy as np
from jax import lax
from jax.experimental import pallas as pl
from jax.experimental.pallas import tpu as pltpu

F32 = jnp.float32
BF16 = jnp.bfloat16

D_MODEL = 1024
PAST_LEN = 2048
PAGE_SIZE = 128
SSM_WIDTH = 512
SSM_GROUP = 16
SSM_GROUPS = 32
SSM_STATE = 64
SSM_LANES = SSM_GROUPS * SSM_STATE
HEAD_DIM = 64
N_HEADS = 8
QK_DIM = 128
ATTN_WIDTH = 1024
ROPE_DIM = 16
ROPE_THETA = 500000.0
D_FF = 2816
IN_WIDTH = SSM_WIDTH + 3 * ATTN_WIDTH + 2 * D_MODEL
EPS = 1e-6
LAM_INIT = 0.8 - 0.6 * math.exp(-0.3 * 0)
Q_SCALE = HEAD_DIM ** -0.5 * math.log2(math.e)

LANES = 128
SUBLANES = 8
MXU_WIDTH = 256
VMEM_LIMIT = 56 * 1024 * 1024
ATTN_VMEM_LIMIT = 62 * 1024 * 1024

PROJ_ROWS = 512
SSM_CHUNK = 128
ATTN_Q_ROWS = 1024
DECODE_GROUPS = 2
ROW_SUBTILES = 2
MIN_SUBTILE_ROWS = MXU_WIDTH


def _params(sem, vmem=VMEM_LIMIT):
    return pltpu.CompilerParams(dimension_semantics=sem, vmem_limit_bytes=vmem)


def _row_subtiles(tm):
    sub = max(tm // ROW_SUBTILES, min(tm, MIN_SUBTILE_ROWS))
    return [slice(r0, r0 + sub) for r0 in range(0, tm, sub)]


def _resident(shape):
    return pl.BlockSpec(shape, lambda *_: (0,) * len(shape), pipeline_mode=pl.Buffered(1))


def _rms(x, g):
    return x * lax.rsqrt(jnp.mean(x * x, axis=-1, keepdims=True) + EPS) * g


def _dot(a, b):
    return jnp.dot(a, b, preferred_element_type=F32)


def _dot_nt(a, b):
    return lax.dot_general(a, b, (((1,), (1,)), ((), ())), preferred_element_type=F32)


def _prep_kernel(lre_ref, lim_ref, ldt_ref, bre_ref, bim_ref, lq1_ref, lk1_ref, lq2_ref, lk2_ref,
                 are_ref, aim_ref, bbre_ref, bbim_ref, lam_ref):
    lre, lim = lre_ref[...], lim_ref[...]
    dt = jnp.exp(ldt_ref[...])
    mag = jnp.exp(lre * dt)
    ar, ai = mag * jnp.cos(lim * dt), mag * jnp.sin(lim * dt)
    den = lre * lre + lim * lim
    fr = ((ar - 1.0) * lre + ai * lim) / den
    fi = (ai * lre - (ar - 1.0) * lim) / den
    are_ref[...] = ar
    aim_ref[...] = ai
    bre, bim = bre_ref[...], bim_ref[...]
    bbre_ref[...] = fr[:, None, :] * bre - fi[:, None, :] * bim
    bbim_ref[...] = fr[:, None, :] * bim + fi[:, None, :] * bre
    s1 = jnp.sum(lq1_ref[...] * lk1_ref[...], axis=-1, keepdims=True)
    s2 = jnp.sum(lq2_ref[...] * lk2_ref[...], axis=-1, keepdims=True)
    lam_ref[...] = jnp.exp(s1) - jnp.exp(s2) + LAM_INIT


def _prep(lam_re, lam_im, log_dt, b_re, b_im, lq1, lk1, lq2, lk2):
    g, p, c = SSM_GROUPS, SSM_STATE, SSM_GROUP
    out_shape = (jax.ShapeDtypeStruct((g, p), F32), jax.ShapeDtypeStruct((g, p), F32),
                 jax.ShapeDtypeStruct((g, c, p), F32), jax.ShapeDtypeStruct((g, c, p), F32),
                 jax.ShapeDtypeStruct((1, 1), F32))
    return pl.pallas_call(_prep_kernel, out_shape=out_shape, name="ssm_prep")(
        lam_re, lam_im, log_dt.reshape(g, 1), jnp.swapaxes(b_re, 1, 2), jnp.swapaxes(b_im, 1, 2),
        lq1.reshape(1, HEAD_DIM), lk1.reshape(1, HEAD_DIM), lq2.reshape(1, HEAD_DIM), lk2.reshape(1, HEAD_DIM))


def _block_diag(blocks):
    g, r, c = blocks.shape
    keep = np.arange(g * r)[:, None] // r == np.arange(g * c)[None, :] // c
    return jnp.where(keep, jnp.tile(blocks.reshape(g * r, c), (1, g)), 0.0)


def _in_proj_kernel(x_ref, g_ref, w_ref, cos_ref, sa_ref, sb_ref,
                    u_ref, q_ref, kf_ref, kb_ref, vf_ref, vb_ref, gs_ref, ga_ref):
    o0 = SSM_WIDTH
    o1, o2, o3, o4 = o0 + ATTN_WIDTH, o0 + 2 * ATTN_WIDTH, o0 + 3 * ATTN_WIDTH, o0 + 3 * ATTN_WIDTH + D_MODEL
    subs = _row_subtiles(x_ref.shape[0])
    sub = subs[0].stop
    heads = [slice(hh * QK_DIM, (hh + 1) * QK_DIM) for hh in range(N_HEADS)]

    def rope(z, rows):
        return (z * cos_ref[rows, :] + pltpu.roll(z, LANES - ROPE_DIM // 2, 1) * sa_ref[rows, :]
                + pltpu.roll(z, ROPE_DIM // 2, 1) * sb_ref[rows, :])

    def seg_u(rows, h):
        u_ref[rows, :] = _dot(h, w_ref[:, 0:o0]).astype(BF16)

    def seg_q(rows, h):
        zq = _dot(h, w_ref[:, o0:o1])
        for sl in heads:
            q_ref[rows, sl] = (rope(zq[:, sl], rows) * Q_SCALE).astype(BF16)

    def seg_k(rows, h):
        zk = _dot(h, w_ref[:, o1:o2])
        for hh, sl in enumerate(heads):
            kr = rope(zk[:, sl], rows)
            kf_ref[pl.ds(rows.start * N_HEADS + hh, sub, stride=N_HEADS), :] = kr
            kb_ref[rows, sl] = kr.astype(BF16)

    def seg_v(rows, h):
        zv = _dot(h, w_ref[:, o2:o3])
        for hh, sl in enumerate(heads):
            vf_ref[pl.ds(rows.start * N_HEADS + hh, sub, stride=N_HEADS), :] = zv[:, sl]
        vb_ref[rows, :] = zv.astype(BF16)

    def seg_gs(rows, h):
        gs_ref[rows, :] = jax.nn.sigmoid(_dot(h, w_ref[:, o3:o4])).astype(BF16)

    def seg_ga(rows, h):
        ga_ref[rows, :] = jax.nn.sigmoid(_dot(h, w_ref[:, o4:IN_WIDTH])).astype(BF16)

    hs = [_rms(x_ref[rows, :], g_ref[...]).astype(BF16) for rows in subs]
    for seg in (seg_u, seg_q, seg_k, seg_v, seg_gs, seg_ga):
        for rows, h in zip(subs, hs):
            seg(rows, h)


def _rope_tables(pos):
    half = ROPE_DIM // 2
    inv = ROPE_THETA ** (-np.arange(half, dtype=np.float64) * 2.0 / ROPE_DIM)
    ang = np.asarray(pos, np.float64)[:, None] * inv[None, :]
    cos, sin = np.cos(ang), np.sin(ang)
    n = ang.shape[0]
    comp_cos = np.concatenate([cos, cos, np.ones((n, HEAD_DIM - ROPE_DIM))], axis=1)
    comp_sa = np.concatenate([-sin, np.zeros((n, HEAD_DIM - half))], axis=1)
    comp_sb = np.concatenate([np.zeros((n, half)), sin, np.zeros((n, HEAD_DIM - ROPE_DIM))], axis=1)
    two = lambda t: jnp.asarray(np.concatenate([t, t], axis=1), F32)
    return two(comp_cos), two(comp_sa), two(comp_sb)


def _in_proj(x2d, pos, nb, g_pre, w_in_bf, tm):
    m = x2d.shape[0]
    t_len = m // nb
    nt = t_len // tm
    cos, sa, sb = _rope_tables(pos)
    row = lambda w: pl.BlockSpec((tm, w), lambda i: (i, 0))
    heads = pl.BlockSpec((tm * N_HEADS, QK_DIM), lambda i: (i, 0))
    tab = pl.BlockSpec((tm, LANES), lambda i: (i % nt, 0))
    out_shape = (jax.ShapeDtypeStruct((t_len, nb * SSM_WIDTH), BF16),
                 jax.ShapeDtypeStruct((m, ATTN_WIDTH), BF16),
                 jax.ShapeDtypeStruct((m * N_HEADS, QK_DIM), F32), jax.ShapeDtypeStruct((m, ATTN_WIDTH), BF16),
                 jax.ShapeDtypeStruct((m * N_HEADS, QK_DIM), F32), jax.ShapeDtypeStruct((m, ATTN_WIDTH), BF16),
                 jax.ShapeDtypeStruct((m, D_MODEL), BF16), jax.ShapeDtypeStruct((m, D_MODEL), BF16))
    out_specs = (pl.BlockSpec((tm, SSM_WIDTH), lambda i: (i % nt, i // nt)),
                 row(ATTN_WIDTH), heads, row(ATTN_WIDTH), heads, row(ATTN_WIDTH),
                 row(D_MODEL), row(D_MODEL))
    return pl.pallas_call(
        _in_proj_kernel, grid=(m // tm,),
        in_specs=[row(D_MODEL), _resident((1, D_MODEL)), _resident((D_MODEL, IN_WIDTH)), tab, tab, tab],
        out_specs=out_specs, out_shape=out_shape,
        compiler_params=_params(("parallel",)), name="in_proj")(x2d, g_pre, w_in_bf, cos, sa, sb)


def _ssm_drive(u, bd_ref, store):
    n_tiles = 2 * SSM_LANES // MXU_WIDTH
    for j in range(n_tiles):
        ks = LANES * ((j % (n_tiles // 2)) // 2)
        lanes = slice(MXU_WIDTH * j, MXU_WIDTH * (j + 1))
        store(lanes, _dot(u[:, ks:ks + LANES], bd_ref[ks:ks + LANES, lanes]))


def _ssm_readout(load, u, cre_ref, cim_ref, dsk_ref):
    half = SSM_LANES // 2
    ys = []
    for j in range(2):
        hre = load(slice(half * j, half * (j + 1))).astype(BF16)
        him = load(slice(SSM_LANES + half * j, SSM_LANES + half * (j + 1))).astype(BF16)
        cols = slice(MXU_WIDTH * j, MXU_WIDTH * (j + 1))
        ys.append(_dot(hre, cre_ref[half * j:half * (j + 1), cols]) + _dot(him, cim_ref[half * j:half * (j + 1), cols]))
    y = jnp.concatenate(ys, axis=1) + dsk_ref[...] * u.astype(F32)
    return jax.nn.gelu(y)


def _glu(yg, wglu_ref):
    return _dot(yg, wglu_ref[:, 0:D_MODEL]) * jax.nn.sigmoid(_dot(yg, wglu_ref[:, D_MODEL:2 * D_MODEL]))


def _ssm_prompt_kernel(u_ref, are_ref, aim_ref, bd_ref, cre_ref, cim_ref, dsk_ref, wglu_ref,
                       ys_ref, hre_ref, him_ref, h_s, il_s, st_re, st_im, *, nb, lc):
    @pl.when(pl.program_id(0) == 0)
    def _():
        st_re[...] = jnp.zeros_like(st_re)
        st_im[...] = jnp.zeros_like(st_im)

    rows = lc * nb
    n_tr = rows // SUBLANES
    n_slabs = SSM_WIDTH // LANES
    slab = lambda j: slice(j * LANES, (j + 1) * LANES)
    seq_rows = lambda b: pl.ds(b, lc, stride=nb)
    for b in range(nb):
        ub = u_ref[:, b * SSM_WIDTH:(b + 1) * SSM_WIDTH].astype(F32)
        for j in range(n_slabs):
            il_s[j, seq_rows(b), :] = ub[:, slab(j)]
    u = jnp.concatenate([il_s[j] for j in range(n_slabs)], axis=1).astype(BF16)

    def store(lanes, val):
        h_s[:, :, lanes] = val.reshape(n_tr, SUBLANES, val.shape[-1])

    _ssm_drive(u, bd_ref, store)

    ar, ai = are_ref[...], aim_ref[...]
    hr, hi = st_re[...], st_im[...]
    for s in range(n_tr):
        for k in range(SUBLANES // nb):
            sub = slice(k * nb, (k + 1) * nb)
            nhr = ar * hr - ai * hi + h_s[s, sub, 0:SSM_LANES]
            nhi = ar * hi + ai * hr + h_s[s, sub, SSM_LANES:2 * SSM_LANES]
            h_s[s, sub, 0:SSM_LANES] = nhr
            h_s[s, sub, SSM_LANES:2 * SSM_LANES] = nhi
            hr, hi = nhr, nhi
    st_re[...] = hr
    st_im[...] = hi
    hre_ref[...] = hr
    him_ref[...] = hi

    load = lambda lanes: h_s[:, :, lanes].reshape(rows, lanes.stop - lanes.start)
    yg = _ssm_readout(load, u, cre_ref, cim_ref, dsk_ref)
    for j in range(n_slabs):
        il_s[j] = yg[:, slab(j)]
    per_seq = [jnp.concatenate([il_s[j, seq_rows(b), :] for j in range(n_slabs)], axis=1) for b in range(nb)]
    res = _glu(jnp.concatenate(per_seq, axis=0).astype(BF16), wglu_ref)
    for b in range(nb):
        ys_ref[b] = res[b * lc:(b + 1) * lc].astype(BF16)


def _ssm_prompt(u_il, nb, t_len, are, aim, bd, cre, cim, dsk, wglu, lc):
    rows = lc * nb
    kern = functools.partial(_ssm_prompt_kernel, nb=nb, lc=lc)
    out_shape = (jax.ShapeDtypeStruct((nb, t_len, D_MODEL), BF16),
                 jax.ShapeDtypeStruct((nb, SSM_LANES), F32), jax.ShapeDtypeStruct((nb, SSM_LANES), F32))
    return pl.pallas_call(
        kern, grid=(t_len // lc,),
        in_specs=[pl.BlockSpec((lc, nb * SSM_WIDTH), lambda c: (c, 0)),
                  _resident((nb, SSM_LANES)), _resident((nb, SSM_LANES)),
                  _resident((SSM_WIDTH, 2 * SSM_LANES)), _resident((SSM_LANES, SSM_WIDTH)),
                  _resident((SSM_LANES, SSM_WIDTH)), _resident((1, SSM_WIDTH)),
                  _resident((SSM_WIDTH, 2 * D_MODEL))],
        out_specs=(pl.BlockSpec((nb, lc, D_MODEL), lambda c: (0, c, 0)),
                   pl.BlockSpec((nb, SSM_LANES), lambda c: (0, 0)), pl.BlockSpec((nb, SSM_LANES), lambda c: (0, 0))),
        out_shape=out_shape,
        scratch_shapes=[pltpu.VMEM((rows // SUBLANES, SUBLANES, 2 * SSM_LANES), F32),
                        pltpu.VMEM((SSM_WIDTH // LANES, rows, LANES), F32),
                        pltpu.VMEM((nb, SSM_LANES), F32), pltpu.VMEM((nb, SSM_LANES), F32)],
        compiler_params=_params(("arbitrary",)), name="ssm_prompt")(
            u_il, jnp.tile(are, (nb, 1)), jnp.tile(aim, (nb, 1)), bd, cre, cim, dsk, wglu)


def _ssm_step_kernel(u_ref, h0re_ref, h0im_ref, are_ref, aim_ref, bd_ref, cre_ref, cim_ref, dsk_ref, wglu_ref,
                     ys_ref, hre_ref, him_ref, h_s):
    u = u_ref[...]

    def store(lanes, val):
        h_s[:, lanes] = val

    _ssm_drive(u, bd_ref, store)
    ar, ai = are_ref[...], aim_ref[...]
    hr, hi = h0re_ref[...], h0im_ref[...]
    nhr = ar * hr - ai * hi + h_s[:, 0:SSM_LANES]
    nhi = ar * hi + ai * hr + h_s[:, SSM_LANES:2 * SSM_LANES]
    hre_ref[...] = nhr
    him_ref[...] = nhi
    h_s[:, 0:SSM_LANES] = nhr
    h_s[:, SSM_LANES:2 * SSM_LANES] = nhi
    yg = _ssm_readout(lambda lanes: h_s[:, lanes], u, cre_ref, cim_ref, dsk_ref)
    ys_ref[...] = _glu(yg.astype(BF16), wglu_ref).astype(BF16)


def _ssm_step(u, h0re, h0im, are, aim, bd, cre, cim, dsk, wglu):
    n = u.shape[0]
    out_shape = (jax.ShapeDtypeStruct((n, D_MODEL), BF16),
                 jax.ShapeDtypeStruct((n, SSM_LANES), F32), jax.ShapeDtypeStruct((n, SSM_LANES), F32))
    return pl.pallas_call(
        _ssm_step_kernel, out_shape=out_shape,
        scratch_shapes=[pltpu.VMEM((n, 2 * SSM_LANES), F32)],
        compiler_params=pltpu.CompilerParams(vmem_limit_bytes=VMEM_LIMIT), name="ssm_step")(
            u, h0re, h0im, are, aim, bd, cre, cim, dsk, wglu)


def _subln(o1, l1, o2, l2, lam, gain):
    o = o1 / l1 - lam * (o2 / l2)
    return _rms(o, gain) * (1.0 - LAM_INIT)


def _prompt_tile(qi, lam_ref, q_ref, k_ref, v_ref, gain_ref, o_ref, sa_s, sb_s, m_s, acc_s, tq, tk):
    lane = lax.broadcasted_iota(jnp.int32, (tk, QK_DIM), 1)
    parts = []
    for half in range(2):
        q = q_ref[half * tk:(half + 1) * tk, :]
        zero = jnp.zeros_like(q)
        parts += [jnp.where(lane < HEAD_DIM, q, zero), jnp.where(lane >= HEAD_DIM, q, zero)]
    qz = jnp.concatenate(parts, axis=0)
    ones = jnp.ones((tk, LANES), BF16)
    wide = lambda a: jnp.concatenate([a] * (tk // LANES), axis=1)

    def scores(c, s_ref, rows=slice(0, 2 * tq)):
        s_ref[rows, :] = _dot_nt(qz[rows], k_ref[c * tk:(c + 1) * tk, :])

    def softmax_pv(c, s_ref, rows=slice(0, 2 * tq), diag_rows=0, first=False):
        s = s_ref[rows, :]
        if diag_rows:
            row = lax.broadcasted_iota(jnp.int32, (diag_rows, tk), 0) % tk
            col = lax.broadcasted_iota(jnp.int32, (diag_rows, tk), 1)
            masked = jnp.where(col <= row, s[0:diag_rows], -jnp.inf)
            s = masked if diag_rows == s.shape[0] else jnp.concatenate([masked, s[diag_rows:]], axis=0)
        m_cur = jnp.max(s, axis=1, keepdims=True)
        v1 = jnp.concatenate([v_ref[c * tk:(c + 1) * tk, :], ones], axis=1)
        if first:
            m_new = jnp.broadcast_to(m_cur, (s.shape[0], LANES))
            acc_s[rows, :] = _dot(jnp.exp2(s - wide(m_new)).astype(BF16), v1)
        else:
            m_prev = m_s[rows, :]
            m_new = jnp.maximum(m_prev, m_cur)
            alpha = jnp.exp2(m_prev - m_new)
            p = jnp.exp2(s - wide(m_new)).astype(BF16)
            acc_s[rows, :] = jnp.concatenate([alpha, alpha], axis=1) * acc_s[rows, :] + _dot(p, v1)
        m_s[rows, :] = m_new

    scores(0, sa_s)
    for j in range(qi):
        c0 = 2 * j
        scores(c0 + 1, sb_s)
        softmax_pv(c0, sa_s, first=j == 0)
        scores(c0 + 2, sa_s)
        softmax_pv(c0 + 1, sb_s)
    c0 = 2 * qi
    late = slice(tq, 2 * tq)
    scores(c0 + 1, sb_s, late)
    softmax_pv(c0, sa_s, diag_rows=tq, first=qi == 0)
    softmax_pv(c0 + 1, sb_s, late, diag_rows=tq)
    acc = acc_s[...]
    lam, gain = lam_ref[0, 0], gain_ref[...]
    for half in range(2):
        a1 = acc[2 * half * tk:(2 * half + 1) * tk]
        a2 = acc[(2 * half + 1) * tk:(2 * half + 2) * tk]
        o_ref[half * tk:(half + 1) * tk, :] = _subln(a1[:, 0:QK_DIM], a1[:, QK_DIM:2 * QK_DIM], a2[:, 0:QK_DIM],
                                                     a2[:, QK_DIM:2 * QK_DIM], lam, gain).astype(BF16)


def _decode_query(q_ref):
    q8 = q_ref[0].astype(F32)
    lane = lax.broadcasted_iota(jnp.int32, (N_HEADS, QK_DIM), 1)
    qz32 = jnp.concatenate([jnp.where(lane < HEAD_DIM, q8, 0.0), jnp.where(lane >= HEAD_DIM, q8, 0.0)], axis=0)
    return qz32, qz32.astype(BF16)


def _decode_groups(q_ref, k_groups, v_groups):
    n_maps = 2 * N_HEADS
    rows_pp = PAGE_SIZE * N_HEADS
    _, qz = _decode_query(q_ref)
    scores = [jnp.concatenate([_dot_nt(qz, r[...].astype(BF16)) for r in ks], axis=1) for ks in k_groups]
    stats, probs = [], []
    for s in scores:
        col = lax.broadcasted_iota(jnp.int32, s.shape, 1)
        row = lax.broadcasted_iota(jnp.int32, s.shape, 0)
        s = jnp.where(col % N_HEADS == row % N_HEADS, s, -jnp.inf)
        m = jnp.max(s, axis=1, keepdims=True)
        p = jnp.exp2(s - m)
        stats.append((m, jnp.sum(p, axis=1, keepdims=True)))
        probs.append(p.astype(BF16))
    out = []
    for (m, l), pb, vs in zip(stats, probs, v_groups):
        pv = _dot(pb[:, 0:rows_pp], vs[0][...].astype(BF16))
        for i in range(1, len(vs)):
            pv = pv + _dot(pb[:, i * rows_pp:(i + 1) * rows_pp], vs[i][...].astype(BF16))
        out.append((m, l, pv))
    return out


def _decode_finish(groups, lam_ref, gain_ref, q_ref, kn_ref, vn_ref, o_ref):
    n_maps = 2 * N_HEADS
    qz32, _ = _decode_query(q_ref)
    two = lambda a: jnp.concatenate([a, a], axis=0)
    s_n = jnp.sum(qz32 * two(kn_ref[0].astype(F32)), axis=1, keepdims=True)
    m = s_n
    for mg, _, _ in groups:
        m = jnp.maximum(m, mg)
    l = jnp.exp2(s_n - m)
    acc = l * two(vn_ref[0].astype(F32))
    for mg, lg, pvg in groups:
        w = jnp.exp2(mg - m)
        l = l + w * lg
        acc = acc + w * pvg
    o_ref[0] = _subln(acc[0:N_HEADS], l[0:N_HEADS], acc[N_HEADS:n_maps], l[N_HEADS:n_maps], lam_ref[0, 0],
                      gain_ref[...]).astype(BF16)


def _attn_kernel(pt_ref, lam_ref, q_ref, k_ref, v_ref, gain_ref, qs_ref, *refs, tq, tk, n_pages):
    del pt_ref
    k_refs, v_refs = refs[0:n_pages], refs[n_pages:2 * n_pages]
    kn_ref, vn_ref, o_ref, os_ref, sa_s, sb_s, m_s, acc_s = refs[2 * n_pages:]
    per = n_pages // DECODE_GROUPS
    split = lambda page_refs: [page_refs[g * per:(g + 1) * per] for g in range(DECODE_GROUPS)]
    for qi in range(k_ref.shape[0] // tq):
        @pl.when(pl.program_id(2) == qi)
        def _(qi=qi):
            groups = _decode_groups(qs_ref, split(k_refs), split(v_refs))
            _decode_finish(groups, lam_ref, gain_ref, qs_ref, kn_ref, vn_ref, os_ref)
            _prompt_tile(qi, lam_ref, q_ref, k_ref, v_ref, gain_ref, o_ref, sa_s, sb_s, m_s, acc_s, tq, tk)


def _attention(page_table, lam, q, k, v, gain, q_s, cache_k, cache_v, k_new, v_new, nb, t_len, tq):
    tk = tq // 2
    n, n_pages = page_table.shape
    nq = t_len // tq
    assert nb * N_HEADS * nq == n, "one decode sample per prompt grid step"
    kern = functools.partial(_attn_kernel, tq=tq, tk=tk, n_pages=n_pages)
    sample = lambda b, h, i: (b * N_HEADS + h) * nq + i
    qspec = pl.BlockSpec((None, tq, QK_DIM), lambda b, h, i, pt: (b, i, h))
    kvspec = pl.BlockSpec((None, t_len, QK_DIM), lambda b, h, i, pt: (b, 0, h))
    row = pl.BlockSpec((1, N_HEADS, QK_DIM), lambda b, h, i, pt: (sample(b, h, i), 0, 0))

    def page(j):
        return pl.BlockSpec((None, PAGE_SIZE * N_HEADS, QK_DIM), lambda b, h, i, pt: (pt[sample(b, h, i), j], 0, 0))

    pages = [page(j) for j in range(n_pages)]
    grid_spec = pltpu.PrefetchScalarGridSpec(
        num_scalar_prefetch=1, grid=(nb, N_HEADS, nq),
        in_specs=[pl.BlockSpec(memory_space=pltpu.SMEM), qspec, kvspec, kvspec,
                  pl.BlockSpec((1, QK_DIM), lambda b, h, i, pt: (0, 0)), row] + pages + pages + [row, row],
        out_specs=(qspec, row),
        scratch_shapes=[pltpu.VMEM((2 * tq, tk), F32), pltpu.VMEM((2 * tq, tk), F32),
                        pltpu.VMEM((2 * tq, LANES), F32), pltpu.VMEM((2 * tq, 2 * QK_DIM), F32)])
    r3 = lambda a: a.reshape(n, N_HEADS, QK_DIM)
    out_shape = (jax.ShapeDtypeStruct((nb, t_len, ATTN_WIDTH), BF16), jax.ShapeDtypeStruct((n, N_HEADS, QK_DIM), BF16))
    return pl.pallas_call(
        kern, grid_spec=grid_spec, out_shape=out_shape,
        compiler_params=_params(("arbitrary", "arbitrary", "arbitrary"), ATTN_VMEM_LIMIT), name="attention")(
            page_table, lam, q, k, v, gain, r3(q_s), *([cache_k] * n_pages), *([cache_v] * n_pages),
            r3(k_new), r3(v_new))


FF_CHUNKS = ((0, 1024), (1024, 2048), (2048, D_FF))


def _out_ffn_kernel(x_ref, ys_ref, o_ref, gs_ref, ga_ref, wo_ref, npm_ref, npf_ref, wg_ref, wu_ref, wd_ref, nof_ref,
                    y_ref):
    subs = _row_subtiles(x_ref.shape[0])
    f32 = lambda ref, rows: ref[rows, :].astype(F32)
    mixed = [(f32(gs_ref, r) * f32(ys_ref, r) + f32(ga_ref, r) * f32(o_ref, r)).astype(BF16) for r in subs]
    proj = [_dot(m, wo_ref[...]) for m in mixed]
    x1 = [x_ref[r, :] + _rms(p, npm_ref[...]) for r, p in zip(subs, proj)]
    hf = [_rms(x, npf_ref[...]).astype(BF16) for x in x1]
    f = [None] * len(subs)
    for c0, c1 in FF_CHUNKS:
        for i, h in enumerate(hf):
            act = (jax.nn.silu(_dot(h, wg_ref[:, c0:c1])) * _dot(h, wu_ref[:, c0:c1])).astype(BF16)
            part = _dot(act, wd_ref[c0:c1, :])
            f[i] = part if f[i] is None else f[i] + part
    for r, x, fi in zip(subs, x1, f):
        y_ref[r, :] = x + _rms(fi, nof_ref[...])


def _out_ffn(x2d, ys, o, gs, ga, wo, npm, npf, wg, wu, wd, nof, tm):
    m = x2d.shape[0]
    row = pl.BlockSpec((tm, D_MODEL), lambda i: (i, 0))
    vec = _resident((1, D_MODEL))
    return pl.pallas_call(
        _out_ffn_kernel, grid=(m // tm,),
        in_specs=[row, row, row, row, row, _resident((D_MODEL, D_MODEL)), vec, vec,
                  _resident((D_MODEL, D_FF)), _resident((D_MODEL, D_FF)), _resident((D_FF, D_MODEL)), vec],
        out_specs=row, out_shape=jax.ShapeDtypeStruct((m, D_MODEL), F32),
        compiler_params=_params(("parallel",)), name="out_ffn")(x2d, ys, o, gs, ga, wo, npm, npf, wg, wu, wd, nof)


def kernel(x_prompt, x_sample, cache_k, cache_v, state_ssm_re, state_ssm_im, page_table, norm_pre_mix, w_in,
           ssm_lambda_re, ssm_lambda_im, ssm_log_dt, ssm_b_re, ssm_b_im, ssm_c_re, ssm_c_im, ssm_d, w_glu_a,
           w_glu_b, lambda_q1, lambda_k1, lambda_q2, lambda_k2, subln_gain, w_o, norm_post_mix, norm_pre_ffn,
           w_gate, w_up, w_down, norm_post_ffn):
    assert w_in.shape[0] == 1, "single-layer stack"
    bp, tp, _ = x_prompt.shape
    bs, ts, _ = x_sample.shape
    assert ts == 1

    are, aim, bbre, bbim, lam = _prep(ssm_lambda_re[0], ssm_lambda_im[0], ssm_log_dt[0], ssm_b_re[0], ssm_b_im[0],
                                      lambda_q1[0], lambda_k1[0], lambda_q2[0], lambda_k2[0])
    are, aim = are.reshape(1, SSM_LANES), aim.reshape(1, SSM_LANES)
    bd = jnp.concatenate([_block_diag(bbre), _block_diag(bbim)], axis=1).astype(BF16)
    cre = _block_diag(jnp.swapaxes(ssm_c_re[0], 1, 2)).astype(BF16)
    cim = _block_diag(jnp.swapaxes(-ssm_c_im[0], 1, 2)).astype(BF16)
    dsk = ssm_d[0].reshape(1, SSM_WIDTH)
    wglu = jnp.concatenate([w_glu_a[0], w_glu_b[0]], axis=1).astype(BF16)
    vec = lambda a: a[0].reshape(1, -1)
    w_in_bf, wo_bf = w_in[0].astype(BF16), w_o[0].astype(BF16)
    wg_bf, wu_bf, wd_bf = w_gate[0].astype(BF16), w_up[0].astype(BF16), w_down[0].astype(BF16)
    gain = vec(subln_gain)
    ssm_w = (are, aim, bd, cre, cim, dsk, wglu)
    ffn_w = (wo_bf, vec(norm_post_mix), vec(norm_pre_ffn), wg_bf, wu_bf, wd_bf, vec(norm_post_ffn))

    xp = x_prompt.reshape(bp * tp, D_MODEL)
    u_il, q, kf, kb, vf, vb, gs, ga = _in_proj(xp, np.arange(tp), bp, vec(norm_pre_mix), w_in_bf, tm=PROJ_ROWS)
    ys, hpr, hpi = _ssm_prompt(u_il, bp, tp, *ssm_w, lc=SSM_CHUNK)

    xs = x_sample.reshape(bs, D_MODEL)
    pos_s = np.full((bs,), PAST_LEN)
    u_s, q_s, kf_s, kb_s, vf_s, vb_s, gs_s, ga_s = _in_proj(xs, pos_s, 1, vec(norm_pre_mix), w_in_bf, tm=bs)
    ys_s, hsr, hsi = _ssm_step(u_s, state_ssm_re[0].reshape(bs, SSM_LANES), state_ssm_im[0].reshape(bs, SSM_LANES),
                               *ssm_w)

    n_phys = cache_k.shape[1]
    pages = lambda c: c[0].reshape(n_phys, PAGE_SIZE * N_HEADS, QK_DIM)
    r3 = lambda a: a.reshape(bp, tp, ATTN_WIDTH)
    o, o_s = _attention(page_table, lam, r3(q), r3(kb), r3(vb), gain, q_s, pages(cache_k), pages(cache_v),
                        kb_s, vb_s, bp, tp, tq=ATTN_Q_ROWS)

    yp = _out_ffn(xp, ys.reshape(bp * tp, D_MODEL), o.reshape(bp * tp, ATTN_WIDTH), gs, ga, *ffn_w, tm=PROJ_ROWS)
    ysamp = _out_ffn(xs, ys_s, o_s.reshape(bs, ATTN_WIDTH), gs_s, ga_s, *ffn_w, tm=bs)

    st = lambda a, n: a.reshape(1, n, SSM_GROUPS, SSM_STATE)
    kv = lambda a, n, t: a.reshape(1, n, t, N_HEADS, QK_DIM)
    return (yp.reshape(bp, tp, D_MODEL), ysamp.reshape(bs, ts, D_MODEL),
            kv(kf, bp, tp), kv(vf, bp, tp), st(hpr, bp), st(hpi, bp),
            kv(kf_s, bs, ts), kv(vf_s, bs, ts), st(hsr, bs), st(hsi, bs))
```
